```python
import math
import jax, jax.numpy as jnp
from jax import lax
import numpy as np

D_MODEL = 1024
BATCH = 16
SEQ = 4096
DEPTH = 1

GRID_W = 64
HY_WIDTH = 512
HY_ORDER = 2
FILT_EMB = 33
FILT_BANDS = (FILT_EMB - 1) // 2
FILT_HID = 64
N_Q_HEADS = 8
N_KV_HEADS = 2
GROUP = N_Q_HEADS // N_KV_HEADS
HEAD_DIM = 64
AXIS_DIM = HEAD_DIM // 2
Q_BLOCK = 128
ROPE_THETA = 10000.0
FF_DIM = 2816
PLE_DIM = 256
EPS = 1e-6

ATT_Q = N_Q_HEADS * HEAD_DIM
ATT_KV = N_KV_HEADS * HEAD_DIM
IN_COLS = 3 * HY_WIDTH + ATT_Q + 2 * ATT_KV + 2 * D_MODEL
SPLITS = [3 * HY_WIDTH,
          3 * HY_WIDTH + ATT_Q,
          3 * HY_WIDTH + ATT_Q + ATT_KV,
          3 * HY_WIDTH + ATT_Q + 2 * ATT_KV,
          3 * HY_WIDTH + ATT_Q + 2 * ATT_KV + D_MODEL]

kernel_name = "hyena_gqa_gated_hybrid_encoder"


def rms_norm(x, g):
    xf = x.astype(jnp.float32)
    y = xf * lax.rsqrt(jnp.mean(xf * xf, axis=-1, keepdims=True) + EPS)
    return (y * g.astype(jnp.float32)).astype(x.dtype)


def swiglu(x, w_gate, w_up, w_down):
    return (jax.nn.silu(x @ w_gate) * (x @ w_up)) @ w_down


def short_conv(z, w, b):
    zp = jnp.pad(z, ((0, 0), (1, 1), (0, 0)))
    return zp[:, :-2] * w[0] + zp[:, 1:-1] * w[1] + zp[:, 2:] * w[2] + b


def hyena_filter_freq(L, w1, b1, f1, w2, b2, f2, w3, deltas):
    f32 = jnp.float32
    t = jnp.linspace(0.0, 1.0, L, dtype=f32)[:, None]
    w = (2.0 * math.pi / L) * jnp.arange(L, dtype=f32)
    bands = jnp.linspace(1e-4, FILT_BANDS - 1, FILT_BANDS, dtype=f32)
    ang = w[:, None] * bands[None, :]
    z = jnp.concatenate([t, jnp.cos(ang), -jnp.sin(ang)], axis=-1)
    h = jnp.sin(f1.astype(f32) * (z @ w1.astype(f32) + b1.astype(f32)))
    h = jnp.sin(f2.astype(f32) * (h @ w2.astype(f32) + b2.astype(f32)))
    h = (h @ w3.astype(f32)).reshape(L, HY_ORDER, 2, HY_WIDTH)
    h = h * jnp.exp(-t[:, :, None, None] * jnp.abs(deltas.astype(f32)))
    h_fwd, h_bwd = h[:, :, 0], h[:, :, 1]
    filt = jnp.concatenate([h_fwd[:1] + h_bwd[:1], h_fwd[1:],
                            jnp.zeros((1, HY_ORDER, HY_WIDTH), f32),
                            h_bwd[:0:-1]], axis=0)
    filt = filt / jnp.sum(jnp.abs(filt), axis=0, keepdims=True)
    return jnp.fft.rfft(filt, axis=0)


def long_conv(z, filt_f, bias):
    L = z.shape[1]
    zf32 = z.astype(jnp.float32)
    zf = jnp.fft.rfft(zf32, n=2 * L, axis=1)
    y = jnp.fft.irfft(zf * filt_f[None], n=2 * L, axis=1)[:, :L]
    return (y + zf32 * bias.astype(jnp.float32)).astype(z.dtype)


def hyena_branch(hy_in, short_w, short_b, filt_f, bias):
    z = short_conv(hy_in, short_w, short_b)
    hv, g1, g2 = jnp.split(z, 3, axis=-1)
    z = g1 * long_conv(hv, filt_f[:, 0], bias[0])
    return g2 * long_conv(z, filt_f[:, 1], bias[1])


def axial_rope(L):
    rows = L // GRID_W
    row = jnp.repeat(jnp.arange(rows, dtype=jnp.float32), GRID_W)
    col = jnp.tile(jnp.arange(GRID_W, dtype=jnp.float32), rows)
    inv = ROPE_THETA ** (-jnp.arange(0, AXIS_DIM, 2, dtype=jnp.float32) / AXIS_DIM)
    ang = jnp.concatenate([row[:, None] * inv, col[:, None] * inv], axis=-1)
    return jnp.cos(ang), jnp.sin(ang)


def apply_rope(x, cos, sin):
    xf = x.astype(jnp.float32).reshape(*x.shape[:-1], HEAD_DIM // 2, 2)
    x0, x1 = xf[..., 0], xf[..., 1]
    c = cos[None, :, None, :]
    s = sin[None, :, None, :]
    out = jnp.stack([x0 * c - x1 * s, x0 * s + x1 * c], axis=-1)
    return out.reshape(x.shape).astype(x.dtype)


def attention_branch(q, k, v, q_gain, k_gain, cos, sin):
    B, L, _ = q.shape
    q = q.reshape(B, L, N_Q_HEADS, HEAD_DIM)
    k = k.reshape(B, L, N_KV_HEADS, HEAD_DIM)
    v = v.reshape(B, L, N_KV_HEADS, HEAD_DIM)
    q = apply_rope(rms_norm(q, q_gain), cos, sin)
    k = apply_rope(rms_norm(k, k_gain), cos, sin)
    n_blocks = L // Q_BLOCK
    qb = q.reshape(B, n_blocks, Q_BLOCK, N_KV_HEADS, GROUP, HEAD_DIM)
    qb = jnp.moveaxis(qb, 1, 0)
    scale = HEAD_DIM ** -0.5

    def attend(q_blk):
        s = jnp.einsum("bqkgd,bskd->bkgqs", q_blk, k,
                       preferred_element_type=jnp.float32) * scale
        w = jax.nn.softmax(s, axis=-1).astype(v.dtype)
        return jnp.einsum("bkgqs,bskd->bqkgd", w, v)

    o = lax.map(attend, qb)
    return jnp.moveaxis(o, 0, 1).reshape(B, L, ATT_Q)


def setup_inputs(seed: int = 0) -> dict:
    key = jax.random.key(seed)
    keys = iter(list(jax.random.split(key, 40)))
    f32 = jnp.float32

    def normal(shape, scale):
        return jax.random.normal(next(keys), shape, f32) * scale

    def gain(n):
        return 1.0 + 0.05 * normal((DEPTH, n), 1.0)

    min_decay = math.log(1e-2) / 1.5
    max_decay = math.log(1e-2) / 0.3
    base_deltas = jnp.linspace(min_decay, max_decay, HY_WIDTH, dtype=f32)

    inputs = {}
    inputs["x"] = normal((BATCH, SEQ, D_MODEL), 1.0)
    inputs["p"] = normal((DEPTH, BATCH, SEQ, PLE_DIM), 1.0)
    inputs["ffn1_norm_pre"] = gain(D_MODEL)
    inputs["ffn1_norm_post"] = gain(D_MODEL)
    inputs["ffn1_w_gate"] = normal((DEPTH, D_MODEL, FF_DIM), D_MODEL ** -0.5)
    inputs["ffn1_w_up"] = normal((DEPTH, D_MODEL, FF_DIM), D_MODEL ** -0.5)
    inputs["ffn1_w_down"] = normal((DEPTH, FF_DIM, D_MODEL), FF_DIM ** -0.5)
    inputs["mix_norm_pre"] = gain(D_MODEL)
    inputs["mix_norm_post"] = gain(D_MODEL)
    inputs["w_in"] = normal((DEPTH, D_MODEL, IN_COLS), D_MODEL ** -0.5)
    inputs["hy_short_w"] = normal((DEPTH, 3, 3 * HY_WIDTH), 3 ** -0.5)
    inputs["hy_short_b"] = normal((DEPTH, 3 * HY_WIDTH), 0.02)
    inputs["filt_w1"] = normal((DEPTH, FILT_EMB, FILT_HID), FILT_EMB ** -0.5)
    inputs["filt_b1"] = normal((DEPTH, FILT_HID), 0.1)
    inputs["filt_freq1"] = 1.0 + 0.01 * normal((DEPTH, FILT_HID), 1.0)
    inputs["filt_w2"] = normal((DEPTH, FILT_HID, FILT_HID), FILT_HID ** -0.5)
    inputs["filt_b2"] = normal((DEPTH, FILT_HID), 0.1)
    inputs["filt_freq2"] = 1.0 + 0.01 * normal((DEPTH, FILT_HID), 1.0)
    inputs["filt_w3"] = normal((DEPTH, FILT_HID, HY_ORDER * 2 * HY_WIDTH), FILT_HID ** -0.5)
    inputs["filt_deltas"] = base_deltas + 0.01 * normal((DEPTH, HY_ORDER, 2, HY_WIDTH), 1.0)
    inputs["hy_bias"] = normal((DEPTH, HY_ORDER, HY_WIDTH), 1.0)
    inputs["q_norm"] = gain(HEAD_DIM)
    inputs["k_norm"] = gain(HEAD_DIM)
    inputs["w_hy_out"] = normal((DEPTH, HY_WIDTH, D_MODEL), HY_WIDTH ** -0.5)
    inputs["w_att_out"] = normal((DEPTH, ATT_Q, D_MODEL), ATT_Q ** -0.5)
    inputs["w_out"] = normal((DEPTH, D_MODEL, D_MODEL), D_MODEL ** -0.5)
    inputs["ffn2_norm_pre"] = gain(D_MODEL)
    inputs["ffn2_norm_post"] = gain(D_MODEL)
    inputs["ffn2_w_gate"] = normal((DEPTH, D_MODEL, FF_DIM), D_MODEL ** -0.5)
    inputs["ffn2_w_up"] = normal((DEPTH, D_MODEL, FF_DIM), D_MODEL ** -0.5)
    inputs["ffn2_w_down"] = normal((DEPTH, FF_DIM, D_MODEL), FF_DIM ** -0.5)
    inputs["ple_norm_pre"] = gain(D_MODEL)
    inputs["ple_norm_post"] = gain(D_MODEL)
    inputs["w_ple_gate"] = normal((DEPTH, D_MODEL, D_MODEL), D_MODEL ** -0.5)
    inputs["w_ple_proj"] = normal((DEPTH, PLE_DIM, D_MODEL), PLE_DIM ** -0.5)
    return inputs


def reference(x, p, ffn1_norm_pre, ffn1_norm_post, ffn1_w_gate, ffn1_w_up, ffn1_w_down,
              mix_norm_pre, mix_norm_post, w_in, hy_short_w, hy_short_b,
              filt_w1, filt_b1, filt_freq1, filt_w2, filt_b2, filt_freq2, filt_w3,
              filt_deltas, hy_bias, q_norm, k_norm, w_hy_out, w_att_out, w_out,
              ffn2_norm_pre, ffn2_norm_post, ffn2_w_gate, ffn2_w_up, ffn2_w_down,
              ple_norm_pre, ple_norm_post, w_ple_gate, w_ple_proj):
    L = x.shape[1]
    rope_cos, rope_sin = axial_rope(L)
    for i in range(DEPTH):
        h = swiglu(rms_norm(x, ffn1_norm_pre[i]), ffn1_w_gate[i], ffn1_w_up[i], ffn1_w_down[i])
        x = x + 0.5 * rms_norm(h, ffn1_norm_post[i])

        u = rms_norm(x, mix_norm_pre[i])
        proj = u @ w_in[i]
        hy_in, q, k, v, gate_a, gate_b = jnp.split(proj, SPLITS, axis=-1)

        filt_f = hyena_filter_freq(L, filt_w1[i], filt_b1[i], filt_freq1[i], filt_w2[i],
                                   filt_b2[i], filt_freq2[i], filt_w3[i], filt_deltas[i])
        y_a = hyena_branch(hy_in, hy_short_w[i], hy_short_b[i], filt_f, hy_bias[i])
        y_b = attention_branch(q, k, v, q_norm[i], k_norm[i], rope_cos, rope_sin)

        merged = (jax.nn.sigmoid(gate_a) * (y_a @ w_hy_out[i])
                  + jax.nn.sigmoid(gate_b) * (y_b @ w_att_out[i]))
        x = x + rms_norm(merged @ w_out[i], mix_norm_post[i])

        h = swiglu(rms_norm(x, ffn2_norm_pre[i]), ffn2_w_gate[i], ffn2_w_up[i], ffn2_w_down[i])
        x = x + 0.5 * rms_norm(h, ffn2_norm_post[i])

        g = jax.nn.sigmoid(rms_norm(x, ple_norm_pre[i]) @ w_ple_gate[i])
        x = x + rms_norm(g * (p[i] @ w_ple_proj[i]), ple_norm_post[i])
    return x
```

```python
import functools
import math

import numpy as np
import jax
import jax.numpy as jnp
from jax import lax
from jax.experimental import pallas as pl
from jax.experimental.pallas import tpu as pltpu

F32 = jnp.float32
BF16 = jnp.bfloat16

GRID_W = 64
HY_WIDTH = 512
HY_ORDER = 2
FILT_EMB = 33
FILT_BANDS = (FILT_EMB - 1) // 2
FILT_HID = 64
N_Q_HEADS = 8
N_KV_HEADS = 2
GROUP = N_Q_HEADS // N_KV_HEADS
HEAD_DIM = 64
AXIS_DIM = HEAD_DIM // 2
ROPE_THETA = 10000.0
EPS = 1e-6
ATT_Q = N_Q_HEADS * HEAD_DIM
ATT_KV = N_KV_HEADS * HEAD_DIM

LANES = 128
SUBLANES = 8
VMEM_LIMIT_BYTES = 60 * 1024 * 1024

NP = 8
ND = 2 * NP - 1


def _cparams(sem):
    return pltpu.CompilerParams(dimension_semantics=sem,
                                vmem_limit_bytes=VMEM_LIMIT_BYTES)


def _const_spec(shape):
    nd = len(shape)
    return pl.BlockSpec(shape, lambda *_: (0,) * nd, pipeline_mode=pl.Buffered(1))


def _rms(x, g):
    return x * lax.rsqrt(jnp.mean(x * x, axis=-1, keepdims=True) + EPS) * g


def _sigmoid(x):
    return 1.0 / (1.0 + jnp.exp(-x))


def _ffn_kernel(x_ref, gpre_ref, gpost_ref, wg_ref, wu_ref, wd_ref, o_ref):
    x = x_ref[...]
    xn = _rms(x, gpre_ref[...]).astype(BF16)
    g = jnp.dot(xn, wg_ref[...], preferred_element_type=F32)
    u = jnp.dot(xn, wu_ref[...], preferred_element_type=F32)
    h = (g * _sigmoid(g) * u).astype(BF16)
    y = jnp.dot(h, wd_ref[...], preferred_element_type=F32)
    o_ref[...] = x + 0.5 * _rms(y, gpost_ref[...])


def _ffn(x2d, gpre, gpost, wg, wu, wd, tm):
    n, d = x2d.shape
    ff = wg.shape[1]
    return pl.pallas_call(
        _ffn_kernel,
        grid=(n // tm,),
        in_specs=[
            pl.BlockSpec((tm, d), lambda i: (i, 0)),
            _const_spec((1, d)), _const_spec((1, d)),
            _const_spec((d, ff)), _const_spec((d, ff)), _const_spec((ff, d)),
        ],
        out_specs=pl.BlockSpec((tm, d), lambda i: (i, 0)),
        out_shape=jax.ShapeDtypeStruct((n, d), F32),
        compiler_params=_cparams(("parallel",)),
        name="ffn",
    )(x2d, gpre, gpost, wg, wu, wd)


def _head_norm_rope(t, gain, bd, cos, sin):
    sq = t * t
    hi = sq.astype(BF16)
    lo = (sq - hi.astype(F32)).astype(BF16)
    ms = (jnp.dot(hi, bd, preferred_element_type=F32)
          + jnp.dot(lo, bd, preferred_element_type=F32))
    tn = t * lax.rsqrt(ms + EPS) * gain
    w = t.shape[-1]
    lane = lax.broadcasted_iota(jnp.int32, tn.shape, 1)
    nxt = pltpu.roll(tn, w - 1, 1)
    prv = pltpu.roll(tn, 1, 1)
    swapped = jnp.where((lane & 1) == 0, nxt, prv)
    return tn * cos + swapped * sin


def _inproj_kernel(x_ref, gpre_ref, w_ref, qg_ref, kg_ref, bdq_ref, bdk_ref,
                   cos_ref, sin_ref, hy_ref, q_ref, k_ref, v_ref):
    nhy = hy_ref.shape[-1]
    u = _rms(x_ref[...], gpre_ref[...]).astype(BF16)
    proj = jnp.dot(u, w_ref[...], preferred_element_type=F32)
    hy_ref[...] = proj[:, :nhy]
    q = proj[:, nhy:nhy + ATT_Q]
    k = proj[:, nhy + ATT_Q:nhy + ATT_Q + ATT_KV]
    v = proj[:, nhy + ATT_Q + ATT_KV:]
    cos_k = cos_ref[...]
    sin_k = sin_ref[...]
    reps = ATT_Q // ATT_KV
    cos_q = jnp.concatenate([cos_k] * reps, axis=1)
    sin_q = jnp.concatenate([sin_k] * reps, axis=1)
    qr = _head_norm_rope(q, qg_ref[...], bdq_ref[...], cos_q, sin_q)
    kr = _head_norm_rope(k, kg_ref[...], bdk_ref[...], cos_k, sin_k)
    q_ref[...] = (qr * (HEAD_DIM ** -0.5)).astype(BF16)
    k_ref[...] = kr.astype(BF16)
    v_ref[...] = v.astype(BF16)


def _inproj(x2d, gpre, w, qg, kg, bdq, bdk, cos_t, sin_t, seq, tm):
    n, d = x2d.shape
    ncols = w.shape[1]
    nhy = ncols - ATT_Q - 2 * ATT_KV
    tiles_per_seq = seq // tm
    return pl.pallas_call(
        _inproj_kernel,
        grid=(n // tm,),
        in_specs=[
            pl.BlockSpec((tm, d), lambda i: (i, 0)),
            _const_spec((1, d)),
            _const_spec((d, ncols)),
            _const_spec((1, ATT_Q)), _const_spec((1, ATT_KV)),
            _const_spec((ATT_Q, ATT_Q)), _const_spec((ATT_KV, ATT_KV)),
            pl.BlockSpec((tm, ATT_KV), lambda i: (i % tiles_per_seq, 0)),
            pl.BlockSpec((tm, ATT_KV), lambda i: (i % tiles_per_seq, 0)),
        ],
        out_specs=[
            pl.BlockSpec((tm, nhy), lambda i: (i, 0)),
            pl.BlockSpec((tm, ATT_Q), lambda i: (i, 0)),
            pl.BlockSpec((tm, ATT_KV), lambda i: (i, 0)),
            pl.BlockSpec((tm, ATT_KV), lambda i: (i, 0)),
        ],
        out_shape=[
            jax.ShapeDtypeStruct((n, nhy), F32),
            jax.ShapeDtypeStruct((n, ATT_Q), BF16),
            jax.ShapeDtypeStruct((n, ATT_KV), BF16),
            jax.ShapeDtypeStruct((n, ATT_KV), BF16),
        ],
        compiler_params=_cparams(("parallel",)),
        name="inproj",
    )(x2d, gpre, w, qg, kg, bdq, bdk, cos_t, sin_t)


def _attn_kernel(q_ref, k_ref, v_ref, o_ref, k0_ref, v0_ref):
    gw = GROUP * HEAD_DIM

    @pl.when(pl.program_id(1) == 0)
    def _():
        lane = lax.broadcasted_iota(jnp.int32, (k_ref.shape[0], gw), 1)
        reps = gw // ATT_KV
        kk = jnp.concatenate([k_ref[...].astype(F32)] * reps, axis=1)
        vv = jnp.concatenate([v_ref[...].astype(F32)] * reps, axis=1)
        for g in range(N_KV_HEADS):
            ks = kk if g == 0 else pltpu.roll(kk, gw - g * HEAD_DIM, 1)
            vs = vv if g == 0 else pltpu.roll(vv, gw - g * HEAD_DIM, 1)
            k0_ref[g] = jnp.where(lane < HEAD_DIM, ks, 0.0).astype(BF16)
            v0_ref[g] = jnp.where(lane < HEAD_DIM, vs, 0.0).astype(BF16)

    for g in range(N_KV_HEADS):
        qg = q_ref[:, g * gw:(g + 1) * gw].astype(F32)
        acc = jnp.zeros((q_ref.shape[0], gw), F32)
        for h in range(GROUP):
            qh = qg if h == 0 else pltpu.roll(qg, gw - h * HEAD_DIM, 1)
            s = lax.dot_general(qh.astype(BF16), k0_ref[g], (((1,), (1,)), ((), ())),
                                preferred_element_type=F32)
            m = jnp.max(s, axis=-1, keepdims=True)
            p = jnp.exp(s - m)
            l = jnp.sum(p, axis=-1, keepdims=True)
            o = jnp.dot(p.astype(BF16), v0_ref[g], preferred_element_type=F32)
            o = o * (1.0 / l)
            acc = acc + (o if h == 0 else pltpu.roll(o, h * HEAD_DIM, 1))
        o_ref[:, g * gw:(g + 1) * gw] = acc.astype(BF16)


def _attention(q, k, v, batch, seq, tq):
    n = q.shape[0]
    gw = GROUP * HEAD_DIM
    qt = seq // tq
    return pl.pallas_call(
        _attn_kernel,
        grid=(batch, qt),
        in_specs=[
            pl.BlockSpec((tq, ATT_Q), lambda b, i: (b * qt + i, 0)),
            pl.BlockSpec((seq, ATT_KV), lambda b, i: (b, 0)),
            pl.BlockSpec((seq, ATT_KV), lambda b, i: (b, 0)),
        ],
        out_specs=pl.BlockSpec((tq, ATT_Q), lambda b, i: (b * qt + i, 0)),
        out_shape=jax.ShapeDtypeStruct((n, ATT_Q), BF16),
        scratch_shapes=[pltpu.VMEM((N_KV_HEADS, seq, gw), BF16),
                        pltpu.VMEM((N_KV_HEADS, seq, gw), BF16)],
        compiler_params=_cparams(("arbitrary", "arbitrary")),
        name="attention",
    )(q, k, v)


def _filt_mlp_kernel(z_ref, w1_ref, b1_ref, f1_ref, w2_ref, b2_ref, f2_ref,
                     w3f_ref, w3b_ref, df_ref, db_ref, hf_ref, hb_ref):
    hp = lax.Precision.HIGHEST
    z = z_ref[...]
    h = jnp.sin(f1_ref[...] * (jnp.dot(z, w1_ref[...], precision=hp,
                                       preferred_element_type=F32) + b1_ref[...]))
    h = jnp.sin(f2_ref[...] * (jnp.dot(h, w2_ref[...], precision=hp,
                                       preferred_element_type=F32) + b2_ref[...]))
    t = z[:, 0:1]
    hf = jnp.dot(h, w3f_ref[...], precision=hp, preferred_element_type=F32)
    hb = jnp.dot(h, w3b_ref[...], precision=hp, preferred_element_type=F32)
    hf = hf * jnp.exp(-t * jnp.abs(df_ref[...]))
    hb = hb * jnp.exp(-t * jnp.abs(db_ref[...]))
    row = lax.broadcasted_iota(jnp.int32, hf.shape, 0)
    both = jnp.abs(hf) + jnp.abs(hb)
    merged = jnp.abs(hf + hb)
    s = jnp.sum(jnp.where(row == 0, merged, both), axis=0, keepdims=True)
    inv = 1.0 / s
    hf_ref[...] = hf * inv
    hb_ref[...] = hb * inv


def _filt_mlp(zf, w1, b1, f1, w2, b2, f2, w3f, w3b, df, db, tc):
    seq, ze = zf.shape
    ncol = w3f.shape[1]
    hid = w2.shape[0]
    col = lambda j: (0, j)
    return pl.pallas_call(
        _filt_mlp_kernel,
        grid=(ncol // tc,),
        in_specs=[
            _const_spec((seq, ze)), _const_spec((ze, hid)), _const_spec((1, hid)),
            _const_spec((1, hid)), _const_spec((hid, hid)), _const_spec((1, hid)),
            _const_spec((1, hid)),
            pl.BlockSpec((hid, tc), col), pl.BlockSpec((hid, tc), col),
            pl.BlockSpec((1, tc), col), pl.BlockSpec((1, tc), col),
        ],
        out_specs=[pl.BlockSpec((seq, tc), col), pl.BlockSpec((seq, tc), col)],
        out_shape=[jax.ShapeDtypeStruct((seq, ncol), F32),
                   jax.ShapeDtypeStruct((seq, ncol), F32)],
        compiler_params=_cparams(("parallel",)),
        name="filt_mlp",
    )(zf, w1, b1, f1, w2, b2, f2, w3f, w3b, df, db)


def _phases(ref, rows):
    return [ref[pl.ds(q, rows, stride=NP), :] for q in range(NP)]


def _split_bf16(x):
    hi = x.astype(BF16)
    lo = (x - hi.astype(F32)).astype(BF16)
    return hi, lo


def _filt_dft_kernel(hf_ref, hb_ref, fcs_ref, cos_ref, sin_ref, p_ref, q_ref):
    rows = hf_ref.shape[0] // NP
    kp = p_ref.shape[1]
    kf = fcs_ref.shape[0] // 2
    xs = _phases(hf_ref, rows) + _phases(hb_ref, rows)
    xcat = jnp.concatenate(xs, axis=1)
    hi, lo = _split_bf16(xcat)
    fcs = fcs_ref[...]
    a = (jnp.dot(fcs, hi, preferred_element_type=F32)
         + jnp.dot(fcs, lo, preferred_element_type=F32))
    c = cos_ref[...]
    s = sin_ref[...]

    def part(dirn, ph):
        j = (dirn * NP + ph) * LANES
        return a[:kp, j:j + LANES], a[kf:kf + kp, j:j + LANES]

    fr0, fi0 = part(0, 0)
    br0, bi0 = part(1, 0)
    p_ref[NP - 1] = fr0 + br0
    q_ref[NP - 1] = fi0 - bi0
    for r in range(1, NP):
        fr, fi = part(0, r)
        br, bi = part(1, NP - r)
        p_ref[NP - 1 + r] = fr + c * br + s * bi
        q_ref[NP - 1 + r] = fi + s * br - c * bi
        fr, fi = part(0, NP - r)
        br, bi = part(1, r)
        p_ref[NP - 1 - r] = c * fr + s * fi + br
        q_ref[NP - 1 - r] = c * fi - s * fr - bi


def _filt_dft(hf, hb, fcs, cos_k, sin_k):
    seq, ncol = hf.shape
    kf = cos_k.shape[0]
    col = lambda j: (0, j)
    out_spec = pl.BlockSpec((ND, kf, LANES), lambda j: (0, 0, j))
    return pl.pallas_call(
        _filt_dft_kernel,
        grid=(ncol // LANES,),
        in_specs=[pl.BlockSpec((seq, LANES), col), pl.BlockSpec((seq, LANES), col),
                  _const_spec(fcs.shape), _const_spec(cos_k.shape),
                  _const_spec(sin_k.shape)],
        out_specs=[out_spec, out_spec],
        out_shape=[jax.ShapeDtypeStruct((ND, kf, ncol), F32),
                   jax.ShapeDtypeStruct((ND, kf, ncol), F32)],
        compiler_params=_cparams(("parallel",)),
        name="filt_dft",
    )(hf, hb, fcs, cos_k, sin_k)


def _short_conv_phases(raw, w_ref, b_ref):
    rows = raw[0].shape[0]
    row = lax.broadcasted_iota(jnp.int32, raw[0].shape, 0)
    before = jnp.where(row == 0, 0.0, pltpu.roll(raw[NP - 1], 1, 0))
    after = jnp.where(row == rows - 1, 0.0, pltpu.roll(raw[0], rows - 1, 0))
    w0 = w_ref[0:1, :]
    w1 = w_ref[1:2, :]
    w2 = w_ref[2:3, :]
    b = b_ref[...]
    out = []
    for q in range(NP):
        prev = raw[q - 1] if q > 0 else before
        nxt = raw[q + 1] if q < NP - 1 else after
        out.append(prev * w0 + raw[q] * w1 + nxt * w2 + b)
    return out


def _longconv_kernel(x_ref, g_ref, wx_ref, bx_ref, wg_ref, bg_ref, bias_ref,
                     p_ref, q_ref, fcs_ref, ginv_ref, o_ref, a_ref, b_ref,
                     *, conv_x):
    rows = x_ref.shape[0] // NP
    kp = p_ref.shape[1]
    kf = fcs_ref.shape[0] // 2
    xs = _phases(x_ref, rows)
    if conv_x:
        xs = _short_conv_phases(xs, wx_ref, bx_ref)
    gs = _short_conv_phases(_phases(g_ref, rows), wg_ref, bg_ref)

    xcat = jnp.concatenate([x.astype(BF16) for x in xs], axis=1)
    a_ref[...] = jnp.dot(fcs_ref[...], xcat, preferred_element_type=F32)

    def body(i, carry):
        r0 = pl.multiple_of(i * SUBLANES, SUBLANES)
        ar = [a_ref[pl.ds(r0, SUBLANES), q * LANES:(q + 1) * LANES] for q in range(NP)]
        ai = [a_ref[pl.ds(kf + r0, SUBLANES), q * LANES:(q + 1) * LANES]
              for q in range(NP)]
        for p in range(NP):
            br = None
            bi = None
            for q in range(NP):
                d = p - q + NP - 1
                pc = p_ref[d, pl.ds(r0, SUBLANES), :]
                qc = q_ref[d, pl.ds(r0, SUBLANES), :]
                tr = pc * ar[q] - qc * ai[q]
                ti = pc * ai[q] + qc * ar[q]
                br = tr if br is None else br + tr
                bi = ti if bi is None else bi + ti
            b_ref[pl.ds(r0, SUBLANES), p * LANES:(p + 1) * LANES] = br
            b_ref[pl.ds(kf + r0, SUBLANES), p * LANES:(p + 1) * LANES] = bi
        return carry

    lax.fori_loop(0, kp // SUBLANES, body, 0)
    if kf > kp:
        zeros = jnp.zeros((kf - kp, NP * LANES), F32)
        b_ref[kp:kf, :] = zeros
        b_ref[kf + kp:, :] = zeros

    y = jnp.dot(ginv_ref[...], b_ref[...].astype(BF16), preferred_element_type=F32)
    bias = bias_ref[...]
    for p in range(NP):
        yp = y[:, p * LANES:(p + 1) * LANES]
        o_ref[pl.ds(p, rows, stride=NP), :] = gs[p] * (yp + xs[p] * bias)


def _longconv(x3, x_blk0, hy3, g_blk0, sw, sb, wx_blk0, bias, pq_blk0, pf, qf,
              fcs, ginv, conv_x):
    batch, seq, _ = hy3.shape
    nch = bias.shape[1] // LANES
    kp = pf.shape[1]
    kernel = functools.partial(_longconv_kernel, conv_x=conv_x)
    return pl.pallas_call(
        kernel,
        grid=(nch, batch),
        in_specs=[
            pl.BlockSpec((None, seq, LANES), lambda c, b: (b, 0, x_blk0 + c)),
            pl.BlockSpec((None, seq, LANES), lambda c, b: (b, 0, g_blk0 + c)),
            pl.BlockSpec((3, LANES), lambda c, b: (0, wx_blk0 + c)),
            pl.BlockSpec((1, LANES), lambda c, b: (0, wx_blk0 + c)),
            pl.BlockSpec((3, LANES), lambda c, b: (0, g_blk0 + c)),
            pl.BlockSpec((1, LANES), lambda c, b: (0, g_blk0 + c)),
            pl.BlockSpec((1, LANES), lambda c, b: (0, c)),
            pl.BlockSpec((ND, kp, LANES), lambda c, b: (0, 0, pq_blk0 + c),
                         pipeline_mode=pl.Buffered(1)),
            pl.BlockSpec((ND, kp, LANES), lambda c, b: (0, 0, pq_blk0 + c),
                         pipeline_mode=pl.Buffered(1)),
            _const_spec(fcs.shape), _const_spec(ginv.shape),
        ],
        out_specs=pl.BlockSpec((None, seq, LANES), lambda c, b: (b, 0, c)),
        out_shape=jax.ShapeDtypeStruct((batch, seq, nch * LANES), F32),
        scratch_shapes=[pltpu.VMEM((fcs.shape[0], NP * LANES), F32),
                        pltpu.VMEM((fcs.shape[0], NP * LANES), F32)],
        compiler_params=_cparams(("arbitrary", "arbitrary")),
        name="longconv_x" if conv_x else "longconv",
    )(x3, hy3, sw, sb, sw, sb, bias, pf, qf, fcs, ginv)


def _merge_kernel(x_ref, ya_ref, yb_ref, gpre_ref, gpost_ref, wgate_ref,
                  wa_ref, wb_ref, wo_ref, o_ref):
    x = x_ref[...]
    d = x.shape[-1]
    u = _rms(x, gpre_ref[...]).astype(BF16)
    gates = jnp.dot(u, wgate_ref[...], preferred_element_type=F32)
    ma = jnp.dot(ya_ref[...].astype(BF16), wa_ref[...], preferred_element_type=F32)
    mb = jnp.dot(yb_ref[...], wb_ref[...], preferred_element_type=F32)
    m = _sigmoid(gates[:, :d]) * ma + _sigmoid(gates[:, d:]) * mb
    y = jnp.dot(m.astype(BF16), wo_ref[...], preferred_element_type=F32)
    o_ref[...] = x + _rms(y, gpost_ref[...])


def _merge(x2d, ya, yb, gpre, gpost, wgate, wa, wb, wo, tm):
    n, d = x2d.shape
    row = lambda i: (i, 0)
    return pl.pallas_call(
        _merge_kernel,
        grid=(n // tm,),
        in_specs=[
            pl.BlockSpec((tm, d), row),
            pl.BlockSpec((tm, ya.shape[1]), row),
            pl.BlockSpec((tm, yb.shape[1]), row),
            _const_spec((1, d)), _const_spec((1, d)),
            _const_spec(wgate.shape), _const_spec(wa.shape),
            _const_spec(wb.shape), _const_spec(wo.shape),
        ],
        out_specs=pl.BlockSpec((tm, d), row),
        out_shape=jax.ShapeDtypeStruct((n, d), F32),
        compiler_params=_cparams(("parallel",)),
        name="merge",
    )(x2d, ya, yb, gpre, gpost, wgate, wa, wb, wo)


def _ple_kernel(x_ref, p_ref, gpre_ref, gpost_ref, wg_ref, wp_ref, o_ref):
    x = x_ref[...]
    u = _rms(x, gpre_ref[...]).astype(BF16)
    g = _sigmoid(jnp.dot(u, wg_ref[...], preferred_element_type=F32))
    e = jnp.dot(p_ref[...].astype(BF16), wp_ref[...], preferred_element_type=F32)
    o_ref[...] = x + _rms(g * e, gpost_ref[...])


def _ple(x2d, p2d, gpre, gpost, wg, wp, tm):
    n, d = x2d.shape
    row = lambda i: (i, 0)
    return pl.pallas_call(
        _ple_kernel,
        grid=(n // tm,),
        in_specs=[
            pl.BlockSpec((tm, d), row),
            pl.BlockSpec((tm, p2d.shape[1]), row),
            _const_spec((1, d)), _const_spec((1, d)),
            _const_spec(wg.shape), _const_spec(wp.shape),
        ],
        out_specs=pl.BlockSpec((tm, d), row),
        out_shape=jax.ShapeDtypeStruct((n, d), F32),
        compiler_params=_cparams(("parallel",)),
        name="ple",
    )(x2d, p2d, gpre, gpost, wg, wp)


def _rope_tables(seq):
    rows = seq // GRID_W
    row = np.repeat(np.arange(rows, dtype=np.float32), GRID_W)
    col = np.tile(np.arange(GRID_W, dtype=np.float32), rows)
    inv = (ROPE_THETA ** (-np.arange(0, AXIS_DIM, 2, dtype=np.float32) / AXIS_DIM)
           ).astype(np.float32)
    ang = np.concatenate([row[:, None] * inv, col[:, None] * inv], axis=-1)
    cos = np.repeat(np.cos(ang), 2, axis=1)
    sin = np.repeat(np.sin(ang), 2, axis=1)
    sign = np.tile(np.array([-1.0, 1.0], np.float32), HEAD_DIM // 2)
    cos_t = np.tile(cos, (1, N_KV_HEADS)).astype(np.float32)
    sin_t = np.tile(sin * sign, (1, N_KV_HEADS)).astype(np.float32)
    return jnp.asarray(cos_t), jnp.asarray(sin_t)


def _block_diag_mean(width):
    idx = np.arange(width) // HEAD_DIM
    return jnp.asarray((idx[:, None] == idx[None, :]).astype(np.float32) / HEAD_DIM,
                       dtype=BF16)


def _filter_features(seq, pad_to):
    t = np.linspace(0.0, 1.0, seq, dtype=np.float32)[:, None]
    w = ((2.0 * math.pi / seq) * np.arange(seq, dtype=np.float32)).astype(np.float32)
    bands = np.linspace(1e-4, FILT_BANDS - 1, FILT_BANDS, dtype=np.float32)
    ang = w[:, None] * bands[None, :]
    z = np.concatenate([t, np.cos(ang), -np.sin(ang)], axis=-1).astype(np.float32)
    return jnp.asarray(np.pad(z, ((0, 0), (0, pad_to - z.shape[1]))))


def _dft_tables(seq):
    m1 = seq // NP
    nf = 2 * m1
    nk = nf // 2 + 1
    kp = -(-nk // SUBLANES) * SUBLANES
    half_tile = LANES // 2
    kf = -(-nk // half_tile) * half_tile
    k = np.arange(nk, dtype=np.float64)[:, None]
    m = np.arange(m1, dtype=np.float64)[None, :]
    th = 2.0 * np.pi * k * m / nf
    fcs = np.zeros((2 * kf, m1), np.float64)
    fcs[:nk] = np.cos(th)
    fcs[kf:kf + nk] = -np.sin(th)
    ck = np.full((nk,), 2.0 / nf)
    ck[0] = 1.0 / nf
    ck[-1] = 1.0 / nf
    ginv = np.zeros((m1, 2 * kf), np.float64)
    ginv[:, :nk] = (np.cos(th) * ck[:, None]).T
    ginv[:, kf:kf + nk] = (-np.sin(th) * ck[:, None]).T
    th1 = 2.0 * np.pi * np.arange(kp, dtype=np.float64) / nf
    th1[nk:] = 0.0
    cos_k = np.repeat(np.cos(th1)[:, None], LANES, axis=1)
    sin_k = np.repeat(np.sin(th1)[:, None], LANES, axis=1)
    return (jnp.asarray(fcs, dtype=F32), jnp.asarray(ginv, dtype=F32),
            jnp.asarray(cos_k, dtype=F32), jnp.asarray(sin_k, dtype=F32))


def _pick_tile(n, want):
    t = min(n, want)
    while n % t:
        t //= 2
    return t


def kernel(x, p, ffn1_norm_pre, ffn1_norm_post, ffn1_w_gate, ffn1_w_up, ffn1_w_down,
           mix_norm_pre, mix_norm_post, w_in, hy_short_w, hy_short_b,
           filt_w1, filt_b1, filt_freq1, filt_w2, filt_b2, filt_freq2, filt_w3,
           filt_deltas, hy_bias, q_norm, k_norm, w_hy_out, w_att_out, w_out,
           ffn2_norm_pre, ffn2_norm_post, ffn2_w_gate, ffn2_w_up, ffn2_w_down,
           ple_norm_pre, ple_norm_post, w_ple_gate, w_ple_proj):
    batch, seq, d = x.shape
    depth = p.shape[0]
    n = batch * seq
    nhy = 3 * HY_WIDTH
    nmix = nhy + ATT_Q + 2 * ATT_KV
    hy_blocks = HY_WIDTH // LANES

    tm = _pick_tile(seq, 512)
    tq = _pick_tile(seq, 256)
    cos_t, sin_t = _rope_tables(seq)
    bdq = _block_diag_mean(ATT_Q)
    bdk = _block_diag_mean(ATT_KV)
    ze = LANES
    zf = _filter_features(seq, ze)
    row = lambda a: a.reshape(1, -1).astype(F32)
    bf = lambda a: a.astype(BF16)
    fcs, ginv, cos_k, sin_k = _dft_tables(seq)
    fcs, ginv = bf(fcs), bf(ginv)

    x2d = x.reshape(n, d)
    for i in range(depth):
        x2d = _ffn(x2d, row(ffn1_norm_pre[i]), row(ffn1_norm_post[i]),
                   bf(ffn1_w_gate[i]), bf(ffn1_w_up[i]), bf(ffn1_w_down[i]), tm)

        hy, q, k, v = _inproj(
            x2d, row(mix_norm_pre[i]), bf(w_in[i][:, :nmix]),
            row(jnp.tile(q_norm[i], N_Q_HEADS)), row(jnp.tile(k_norm[i], N_KV_HEADS)),
            bdq, bdk, cos_t, sin_t, seq, tm)

        w3 = filt_w3[i].reshape(FILT_HID, HY_ORDER, 2, HY_WIDTH)
        dl = filt_deltas[i]
        w1p = jnp.pad(filt_w1[i], ((0, ze - FILT_EMB), (0, 0)))
        hf, hb = _filt_mlp(
            zf, w1p, row(filt_b1[i]), row(filt_freq1[i]), filt_w2[i],
            row(filt_b2[i]), row(filt_freq2[i]),
            w3[:, :, 0].reshape(FILT_HID, -1), w3[:, :, 1].reshape(FILT_HID, -1),
            row(dl[:, 0]), row(dl[:, 1]), LANES)
        pf, qf = _filt_dft(hf, hb, fcs, cos_k, sin_k)

        hy3 = hy.reshape(batch, seq, nhy)
        sw = hy_short_w[i]
        sb = row(hy_short_b[i])
        z1 = _longconv(hy3, 0, hy3, hy_blocks, sw, sb, 0, row(hy_bias[i][0]),
                       0, pf, qf, fcs, ginv, True)
        ya = _longconv(z1, 0, hy3, 2 * hy_blocks, sw, sb, 0, row(hy_bias[i][1]),
                       hy_blocks, pf, qf, fcs, ginv, False)

        yb = _attention(q, k, v, batch, seq, tq)

        x2d = _merge(x2d, ya.reshape(n, HY_WIDTH), yb, row(mix_norm_pre[i]),
                     row(mix_norm_post[i]), bf(w_in[i][:, nmix:]), bf(w_hy_out[i]),
                     bf(w_att_out[i]), bf(w_out[i]), tm)

        x2d = _ffn(x2d, row(ffn2_norm_pre[i]), row(ffn2_norm_post[i]),
                   bf(ffn2_w_gate[i]), bf(ffn2_w_up[i]), bf(ffn2_w_down[i]), tm)

        x2d = _ple(x2d, p[i].reshape(n, -1), row(ple_norm_pre[i]),
                   row(ple_norm_post[i]), bf(w_ple_gate[i]), bf(w_ple_proj[i]), tm)
    return x2d.reshape(batch, seq, d)
```

```python
import functools
import math

import numpy as np
import jax
import jax.numpy as jnp
from jax import lax
from jax.experimental import pallas as pl
from jax.experimental.pallas import tpu as pltpu

F32 = jnp.float32
BF16 = jnp.bfloat16

GRID_W = 64
HY_WIDTH = 512
HY_ORDER = 2
FILT_EMB = 33
FILT_BANDS = (FILT_EMB - 1) // 2
FILT_HID = 64
N_Q_HEADS = 8
N_KV_HEADS = 2
GROUP = N_Q_HEADS // N_KV_HEADS
HEAD_DIM = 64
AXIS_DIM = HEAD_DIM // 2
ROPE_THETA = 10000.0
EPS = 1e-6
ATT_Q = N_Q_HEADS * HEAD_DIM
ATT_KV = N_KV_HEADS * HEAD_DIM

LANES = 128
SUBLANES = 8
VMEM_LIMIT_BYTES = 60 * 1024 * 1024

NP = 8


def _cparams(sem):
    return pltpu.CompilerParams(dimension_semantics=sem,
                                vmem_limit_bytes=VMEM_LIMIT_BYTES)


def _const_spec(shape):
    nd = len(shape)
    return pl.BlockSpec(shape, lambda *_: (0,) * nd, pipeline_mode=pl.Buffered(1))


def _rms(x, g):
    return x * lax.rsqrt(jnp.mean(x * x, axis=-1, keepdims=True) + EPS) * g


def _sigmoid(x):
    return 1.0 / (1.0 + jnp.exp(-x))


def _ffn_kernel(x_ref, gpre_ref, gpost_ref, wg_ref, wu_ref, wd_ref, o_ref):
    x = x_ref[...]
    xn = _rms(x, gpre_ref[...]).astype(BF16)
    g = jnp.dot(xn, wg_ref[...], preferred_element_type=F32)
    u = jnp.dot(xn, wu_ref[...], preferred_element_type=F32)
    h = (g * _sigmoid(g) * u).astype(BF16)
    y = jnp.dot(h, wd_ref[...], preferred_element_type=F32)
    o_ref[...] = x + 0.5 * _rms(y, gpost_ref[...])


def _ffn(x2d, gpre, gpost, wg, wu, wd, tm):
    n, d = x2d.shape
    ff = wg.shape[1]
    return pl.pallas_call(
        _ffn_kernel,
        grid=(n // tm,),
        in_specs=[
            pl.BlockSpec((tm, d), lambda i: (i, 0)),
            _const_spec((1, d)), _const_spec((1, d)),
            _const_spec((d, ff)), _const_spec((d, ff)), _const_spec((ff, d)),
        ],
        out_specs=pl.BlockSpec((tm, d), lambda i: (i, 0)),
        out_shape=jax.ShapeDtypeStruct((n, d), F32),
        compiler_params=_cparams(("parallel",)),
        name="ffn",
    )(x2d, gpre, gpost, wg, wu, wd)


Q_SCALE = HEAD_DIM ** -0.5 * math.log2(math.e)


def _head_norm_rope(t, gain, bd, cos, sin):
    ms = jnp.dot((t * t).astype(BF16), bd, preferred_element_type=F32)
    tn = t * lax.rsqrt(ms + EPS) * gain
    w = t.shape[-1]
    lane = lax.broadcasted_iota(jnp.int32, tn.shape, 1)
    nxt = pltpu.roll(tn, w - 1, 1)
    prv = pltpu.roll(tn, 1, 1)
    swapped = jnp.where((lane & 1) == 0, nxt, prv)
    return tn * cos + swapped * sin


def _inproj_kernel(x_ref, gpre_ref, w_ref, qg_ref, kg_ref, bdq_ref, bdk_ref,
                   cos_ref, sin_ref, hy_ref, q_ref, k_ref, v_ref):
    nhy = hy_ref.shape[-1]
    u = _rms(x_ref[...], gpre_ref[...]).astype(BF16)
    proj = jnp.dot(u, w_ref[...], preferred_element_type=F32)
    hy_ref[...] = proj[:, :nhy]
    q = proj[:, nhy:nhy + ATT_Q]
    k = proj[:, nhy + ATT_Q:nhy + ATT_Q + ATT_KV]
    v = proj[:, nhy + ATT_Q + ATT_KV:]
    cos_k = cos_ref[...]
    sin_k = sin_ref[...]
    reps = ATT_Q // ATT_KV
    cos_q = jnp.concatenate([cos_k] * reps, axis=1)
    sin_q = jnp.concatenate([sin_k] * reps, axis=1)
    qr = _head_norm_rope(q, qg_ref[...], bdq_ref[...], cos_q, sin_q)
    kr = _head_norm_rope(k, kg_ref[...], bdk_ref[...], cos_k, sin_k)
    q_ref[...] = (qr * Q_SCALE).astype(BF16)
    k_ref[...] = kr.astype(BF16)
    v_ref[...] = v.astype(BF16)


def _inproj(x2d, gpre, w, qg, kg, bdq, bdk, cos_t, sin_t, seq, tm):
    n, d = x2d.shape
    ncols = w.shape[1]
    nhy = ncols - ATT_Q - 2 * ATT_KV
    tiles_per_seq = seq // tm
    return pl.pallas_call(
        _inproj_kernel,
        grid=(n // tm,),
        in_specs=[
            pl.BlockSpec((tm, d), lambda i: (i, 0)),
            _const_spec((1, d)),
            _const_spec((d, ncols)),
            _const_spec((1, ATT_Q)), _const_spec((1, ATT_KV)),
            _const_spec((ATT_Q, ATT_Q)), _const_spec((ATT_KV, ATT_KV)),
            pl.BlockSpec((tm, ATT_KV), lambda i: (i % tiles_per_seq, 0)),
            pl.BlockSpec((tm, ATT_KV), lambda i: (i % tiles_per_seq, 0)),
        ],
        out_specs=[
            pl.BlockSpec((tm, nhy), lambda i: (i, 0)),
            pl.BlockSpec((tm, ATT_Q), lambda i: (i, 0)),
            pl.BlockSpec((tm, ATT_KV), lambda i: (i, 0)),
            pl.BlockSpec((tm, ATT_KV), lambda i: (i, 0)),
        ],
        out_shape=[
            jax.ShapeDtypeStruct((n, nhy), F32),
            jax.ShapeDtypeStruct((n, ATT_Q), BF16),
            jax.ShapeDtypeStruct((n, ATT_KV), BF16),
            jax.ShapeDtypeStruct((n, ATT_KV), BF16),
        ],
        compiler_params=_cparams(("parallel",)),
        name="inproj",
    )(x2d, gpre, w, qg, kg, bdq, bdk, cos_t, sin_t)


def _attn_kernel(q_ref, k_ref, v_ref, o_ref, k0_ref, v0_ref):
    gw = GROUP * HEAD_DIM

    @pl.when(pl.program_id(1) == 0)
    def _():
        lane = lax.broadcasted_iota(jnp.int32, (k_ref.shape[0], gw), 1)
        reps = gw // ATT_KV
        kk = jnp.concatenate([k_ref[...].astype(F32)] * reps, axis=1)
        vv = jnp.concatenate([v_ref[...].astype(F32)] * reps, axis=1)
        for g in range(N_KV_HEADS):
            ks = kk if g == 0 else pltpu.roll(kk, gw - g * HEAD_DIM, 1)
            vs = vv if g == 0 else pltpu.roll(vv, gw - g * HEAD_DIM, 1)
            k0_ref[g] = jnp.where(lane < HEAD_DIM, ks, 0.0).astype(BF16)
            v0_ref[g] = jnp.where(lane < HEAD_DIM, vs,
                                  jnp.where(lane == HEAD_DIM, 1.0, 0.0)).astype(BF16)

    lane_o = lax.broadcasted_iota(jnp.int32, (q_ref.shape[0], gw), 1)
    for g in range(N_KV_HEADS):
        qg = q_ref[:, g * gw:(g + 1) * gw].astype(F32)
        acc = jnp.zeros((q_ref.shape[0], gw), F32)
        for h in range(GROUP):
            qh = qg if h == 0 else pltpu.roll(qg, gw - h * HEAD_DIM, 1)
            s = lax.dot_general(qh.astype(BF16), k0_ref[g], (((1,), (1,)), ((), ())),
                                preferred_element_type=F32)
            m = jnp.max(s, axis=-1, keepdims=True)
            p = jnp.exp2(s - m)
            o = jnp.dot(p.astype(BF16), v0_ref[g], preferred_element_type=F32)
            l = o[:, HEAD_DIM:HEAD_DIM + 1]
            o = jnp.where(lane_o < HEAD_DIM, o * (1.0 / l), 0.0)
            acc = acc + (o if h == 0 else pltpu.roll(o, h * HEAD_DIM, 1))
        o_ref[:, g * gw:(g + 1) * gw] = acc.astype(BF16)


def _attention(q, k, v, batch, seq, tq):
    n = q.shape[0]
    gw = GROUP * HEAD_DIM
    qt = seq // tq
    return pl.pallas_call(
        _attn_kernel,
        grid=(batch, qt),
        in_specs=[
            pl.BlockSpec((tq, ATT_Q), lambda b, i: (b * qt + i, 0)),
            pl.BlockSpec((seq, ATT_KV), lambda b, i: (b, 0)),
            pl.BlockSpec((seq, ATT_KV), lambda b, i: (b, 0)),
        ],
        out_specs=pl.BlockSpec((tq, ATT_Q), lambda b, i: (b * qt + i, 0)),
        out_shape=jax.ShapeDtypeStruct((n, ATT_Q), BF16),
        scratch_shapes=[pltpu.VMEM((N_KV_HEADS, seq, gw), BF16),
                        pltpu.VMEM((N_KV_HEADS, seq, gw), BF16)],
        compiler_params=_cparams(("arbitrary", "arbitrary")),
        name="attention",
    )(q, k, v)


def _filt_mlp_kernel(z_ref, w1_ref, b1_ref, f1_ref, w2_ref, b2_ref, f2_ref,
                     w3f_ref, w3b_ref, df_ref, db_ref, hf_ref, hb_ref, h_ref):
    hp = lax.Precision.HIGHEST

    @pl.when(pl.program_id(0) == 0)
    def _():
        h1 = jnp.sin(f1_ref[...] * (jnp.dot(z_ref[...], w1_ref[...], precision=hp,
                                            preferred_element_type=F32) + b1_ref[...]))
        h_ref[...] = jnp.sin(f2_ref[...] * (jnp.dot(h1, w2_ref[...], precision=hp,
                                                    preferred_element_type=F32)
                                            + b2_ref[...]))

    h = h_ref[...]
    t = z_ref[:, 0:1]
    hf = jnp.dot(h, w3f_ref[...], precision=hp, preferred_element_type=F32)
    hb = jnp.dot(h, w3b_ref[...], precision=hp, preferred_element_type=F32)
    hf = hf * jnp.exp(-t * jnp.abs(df_ref[...]))
    hb = hb * jnp.exp(-t * jnp.abs(db_ref[...]))
    row = lax.broadcasted_iota(jnp.int32, hf.shape, 0)
    both = jnp.abs(hf) + jnp.abs(hb)
    merged = jnp.abs(hf + hb)
    s = jnp.sum(jnp.where(row == 0, merged, both), axis=0, keepdims=True)
    inv = 1.0 / s
    hf_ref[...] = hf * inv
    hb_ref[...] = hb * inv


def _filt_mlp(zf, w1, b1, f1, w2, b2, f2, w3f, w3b, df, db, tc):
    seq, ze = zf.shape
    ncol = w3f.shape[1]
    hid = w2.shape[0]
    col = lambda j: (0, j)
    return pl.pallas_call(
        _filt_mlp_kernel,
        grid=(ncol // tc,),
        in_specs=[
            _const_spec((seq, ze)), _const_spec((ze, hid)), _const_spec((1, hid)),
            _const_spec((1, hid)), _const_spec((hid, hid)), _const_spec((1, hid)),
            _const_spec((1, hid)),
            pl.BlockSpec((hid, tc), col), pl.BlockSpec((hid, tc), col),
            pl.BlockSpec((1, tc), col), pl.BlockSpec((1, tc), col),
        ],
        out_specs=[pl.BlockSpec((seq, tc), col), pl.BlockSpec((seq, tc), col)],
        out_shape=[jax.ShapeDtypeStruct((seq, ncol), F32),
                   jax.ShapeDtypeStruct((seq, ncol), F32)],
        scratch_shapes=[pltpu.VMEM((seq, hid), F32)],
        compiler_params=_cparams(("arbitrary",)),
        name="filt_mlp",
    )(zf, w1, b1, f1, w2, b2, f2, w3f, w3b, df, db)


def _phases(ref, rows):
    return [ref[pl.ds(q, rows, stride=NP), :] for q in range(NP)]


def _split_bf16(x):
    hi = x.astype(BF16)
    lo = (x - hi.astype(F32)).astype(BF16)
    return hi, lo


def _cmul_tw(xr, xi, c, s, conj):
    if conj:
        return xr * c - xi * s, xi * c + xr * s
    return xr * c + xi * s, xi * c - xr * s


def _fft_list(xs, sign):
    n = len(xs)
    if n == 1:
        return xs
    even = _fft_list(xs[0::2], sign)
    odd = _fft_list(xs[1::2], sign)
    half = n // 2
    out = [None] * n
    for k in range(half):
        er, ei = even[k]
        o_r, o_i = odd[k]
        if k == 0:
            tr, ti = o_r, o_i
            out[k] = (er + tr, ei + ti)
            out[k + half] = (er - tr, ei - ti)
        elif 4 * k == n:
            if sign > 0:
                out[k] = (er - o_i, ei + o_r)
                out[k + half] = (er + o_i, ei - o_r)
            else:
                out[k] = (er + o_i, ei - o_r)
                out[k + half] = (er - o_i, ei + o_r)
        else:
            c = math.cos(2.0 * math.pi * k / n)
            s = sign * math.sin(2.0 * math.pi * k / n)
            tr = o_r * c - o_i * s
            ti = o_r * s + o_i * c
            out[k] = (er + tr, ei + ti)
            out[k + half] = (er - tr, ei - ti)
    return out


def _filt_dft_kernel(hf_ref, hb_ref, fcs_ref, cos_ref, sin_ref, twc_ref, tws_ref,
                     hr_ref, hi_ref):
    rows = hf_ref.shape[0] // NP
    kp = hr_ref.shape[1]
    kf = fcs_ref.shape[0] // 2
    xs = _phases(hf_ref, rows) + _phases(hb_ref, rows)
    xcat = jnp.concatenate(xs, axis=1)
    hi, lo = _split_bf16(xcat)
    fcs = fcs_ref[...]
    a = (jnp.dot(fcs, hi, preferred_element_type=F32)
         + jnp.dot(fcs, lo, preferred_element_type=F32))
    c = cos_ref[...]
    s = sin_ref[...]

    def part(dirn, ph):
        j = (dirn * NP + ph) * LANES
        return a[:kp, j:j + LANES], a[kf:kf + kp, j:j + LANES]

    fr0, fi0 = part(0, 0)
    br0, bi0 = part(1, 0)
    gs = [(fr0 + br0, fi0 - bi0)]
    for d in range(1, NP):
        fr, fi = part(0, d)
        br, bi = part(1, NP - d)
        gr = fr + c * br + s * bi
        gi = fi + s * br - c * bi
        gs.append(_cmul_tw(gr, gi, twc_ref[d], tws_ref[d], False))
    hs = _fft_list(gs, -1)
    for k2 in range(NP):
        hr_ref[k2] = hs[k2][0] * (1.0 / NP)
        hi_ref[k2] = hs[k2][1] * (1.0 / NP)


def _filt_dft(hf, hb, fcs, cos_k, sin_k, twc, tws):
    seq, ncol = hf.shape
    kp = cos_k.shape[0]
    col = lambda j: (0, j)
    out_spec = pl.BlockSpec((NP, kp, LANES), lambda j: (0, 0, j))
    return pl.pallas_call(
        _filt_dft_kernel,
        grid=(ncol // LANES,),
        in_specs=[pl.BlockSpec((seq, LANES), col), pl.BlockSpec((seq, LANES), col),
                  _const_spec(fcs.shape), _const_spec(cos_k.shape),
                  _const_spec(sin_k.shape), _const_spec(twc.shape),
                  _const_spec(tws.shape)],
        out_specs=[out_spec, out_spec],
        out_shape=[jax.ShapeDtypeStruct((NP, kp, ncol), F32),
                   jax.ShapeDtypeStruct((NP, kp, ncol), F32)],
        compiler_params=_cparams(("parallel",)),
        name="filt_dft",
    )(hf, hb, fcs, cos_k, sin_k, twc, tws)


def _short_conv_phases(raw, w_ref, b_ref):
    rows = raw[0].shape[0]
    row = lax.broadcasted_iota(jnp.int32, raw[0].shape, 0)
    before = jnp.where(row == 0, 0.0, pltpu.roll(raw[NP - 1], 1, 0))
    after = jnp.where(row == rows - 1, 0.0, pltpu.roll(raw[0], rows - 1, 0))
    w0 = w_ref[0:1, :]
    w1 = w_ref[1:2, :]
    w2 = w_ref[2:3, :]
    b = b_ref[...]
    out = []
    for q in range(NP):
        prev = raw[q - 1] if q > 0 else before
        nxt = raw[q + 1] if q < NP - 1 else after
        out.append(prev * w0 + raw[q] * w1 + nxt * w2 + b)
    return out


def _longconv_kernel(x_ref, g_ref, wx_ref, bx_ref, wg_ref, bg_ref, bias_ref,
                     hr_ref, hi_ref, twc_ref, tws_ref, fcs_ref, ginv_ref, o_ref,
                     a_ref, b_ref, *, conv_x):
    rows = x_ref.shape[0] // NP
    kp = hr_ref.shape[1]
    kf = fcs_ref.shape[0] // 2
    xs = _phases(x_ref, rows)
    if conv_x:
        xs = _short_conv_phases(xs, wx_ref, bx_ref)
    gs = _short_conv_phases(_phases(g_ref, rows), wg_ref, bg_ref)

    xcat = jnp.concatenate([x.astype(BF16) for x in xs], axis=1)
    a_ref[...] = jnp.dot(fcs_ref[...], xcat, preferred_element_type=F32)

    def body(i, carry):
        r0 = pl.multiple_of(i * SUBLANES, SUBLANES)
        rs = pl.ds(r0, SUBLANES)
        zs = []
        for q in range(NP):
            ar = a_ref[rs, q * LANES:(q + 1) * LANES]
            ai = a_ref[pl.ds(kf + r0, SUBLANES), q * LANES:(q + 1) * LANES]
            zs.append((ar, ai) if q == 0 else
                      _cmul_tw(ar, ai, twc_ref[q, rs, :], tws_ref[q, rs, :], False))
        zs = _fft_list(zs, -1)
        ys = []
        for k2 in range(NP):
            zr, zi = zs[k2]
            hr = hr_ref[k2, rs, :]
            hi = hi_ref[k2, rs, :]
            ys.append((hr * zr - hi * zi, hr * zi + hi * zr))
        ys = _fft_list(ys, 1)
        for p in range(NP):
            br, bi = ys[p]
            if p > 0:
                br, bi = _cmul_tw(br, bi, twc_ref[p, rs, :], tws_ref[p, rs, :], True)
            b_ref[rs, p * LANES:(p + 1) * LANES] = br
            b_ref[pl.ds(kf + r0, SUBLANES), p * LANES:(p + 1) * LANES] = bi
        return carry

    lax.fori_loop(0, kp // SUBLANES, body, 0)
    if kf > kp:
        zeros = jnp.zeros((kf - kp, NP * LANES), F32)
        b_ref[kp:kf, :] = zeros
        b_ref[kf + kp:, :] = zeros

    y = jnp.dot(ginv_ref[...], b_ref[...].astype(BF16), preferred_element_type=F32)
    bias = bias_ref[...]
    for p in range(NP):
        yp = y[:, p * LANES:(p + 1) * LANES]
        o_ref[pl.ds(p, rows, stride=NP), :] = gs[p] * (yp + xs[p] * bias)


def _longconv(x3, x_blk0, hy3, g_blk0, sw, sb, wx_blk0, bias, h_blk0, hr, hi,
              twc, tws, fcs, ginv, conv_x):
    batch, seq, _ = hy3.shape
    nch = bias.shape[1] // LANES
    kp = hr.shape[1]
    kernel = functools.partial(_longconv_kernel, conv_x=conv_x)
    return pl.pallas_call(
        kernel,
        grid=(nch, batch),
        in_specs=[
            pl.BlockSpec((None, seq, LANES), lambda c, b: (b, 0, x_blk0 + c)),
            pl.BlockSpec((None, seq, LANES), lambda c, b: (b, 0, g_blk0 + c)),
            pl.BlockSpec((3, LANES), lambda c, b: (0, wx_blk0 + c)),
            pl.BlockSpec((1, LANES), lambda c, b: (0, wx_blk0 + c)),
            pl.BlockSpec((3, LANES), lambda c, b: (0, g_blk0 + c)),
            pl.BlockSpec((1, LANES), lambda c, b: (0, g_blk0 + c)),
            pl.BlockSpec((1, LANES), lambda c, b: (0, c)),
            pl.BlockSpec((NP, kp, LANES), lambda c, b: (0, 0, h_blk0 + c),
                         pipeline_mode=pl.Buffered(1)),
            pl.BlockSpec((NP, kp, LANES), lambda c, b: (0, 0, h_blk0 + c),
                         pipeline_mode=pl.Buffered(1)),
            _const_spec(twc.shape), _const_spec(tws.shape),
            _const_spec(fcs.shape), _const_spec(ginv.shape),
        ],
        out_specs=pl.BlockSpec((None, seq, LANES), lambda c, b: (b, 0, c)),
        out_shape=jax.ShapeDtypeStruct((batch, seq, nch * LANES), F32),
        scratch_shapes=[pltpu.VMEM((fcs.shape[0], NP * LANES), F32),
                        pltpu.VMEM((fcs.shape[0], NP * LANES), F32)],
        compiler_params=_cparams(("arbitrary", "arbitrary")),
        name="longconv_x" if conv_x else "longconv",
    )(x3, hy3, sw, sb, sw, sb, bias, hr, hi, twc, tws, fcs, ginv)


def _merge_kernel(x_ref, ya_ref, yb_ref, gpre_ref, gpost_ref, wgate_ref,
                  wa_ref, wb_ref, wo_ref, o_ref):
    x = x_ref[...]
    d = x.shape[-1]
    u = _rms(x, gpre_ref[...]).astype(BF16)
    gates = jnp.dot(u, wgate_ref[...], preferred_element_type=F32)
    ma = jnp.dot(ya_ref[...].astype(BF16), wa_ref[...], preferred_element_type=F32)
    mb = jnp.dot(yb_ref[...], wb_ref[...], preferred_element_type=F32)
    m = _sigmoid(gates[:, :d]) * ma + _sigmoid(gates[:, d:]) * mb
    y = jnp.dot(m.astype(BF16), wo_ref[...], preferred_element_type=F32)
    o_ref[...] = x + _rms(y, gpost_ref[...])


def _merge(x2d, ya, yb, gpre, gpost, wgate, wa, wb, wo, tm):
    n, d = x2d.shape
    row = lambda i: (i, 0)
    return pl.pallas_call(
        _merge_kernel,
        grid=(n // tm,),
        in_specs=[
            pl.BlockSpec((tm, d), row),
            pl.BlockSpec((tm, ya.shape[1]), row),
            pl.BlockSpec((tm, yb.shape[1]), row),
            _const_spec((1, d)), _const_spec((1, d)),
            _const_spec(wgate.shape), _const_spec(wa.shape),
            _const_spec(wb.shape), _const_spec(wo.shape),
        ],
        out_specs=pl.BlockSpec((tm, d), row),
        out_shape=jax.ShapeDtypeStruct((n, d), F32),
        compiler_params=_cparams(("parallel",)),
        name="merge",
    )(x2d, ya, yb, gpre, gpost, wgate, wa, wb, wo)


def _ple_kernel(x_ref, p_ref, gpre_ref, gpost_ref, wg_ref, wp_ref, o_ref):
    x = x_ref[...]
    u = _rms(x, gpre_ref[...]).astype(BF16)
    g = _sigmoid(jnp.dot(u, wg_ref[...], preferred_element_type=F32))
    e = jnp.dot(p_ref[...].astype(BF16), wp_ref[...], preferred_element_type=F32)
    o_ref[...] = x + _rms(g * e, gpost_ref[...])


def _ple(x2d, p2d, gpre, gpost, wg, wp, tm):
    n, d = x2d.shape
    row = lambda i: (i, 0)
    return pl.pallas_call(
        _ple_kernel,
        grid=(n // tm,),
        in_specs=[
            pl.BlockSpec((tm, d), row),
            pl.BlockSpec((tm, p2d.shape[1]), row),
            _const_spec((1, d)), _const_spec((1, d)),
            _const_spec(wg.shape), _const_spec(wp.shape),
        ],
        out_specs=pl.BlockSpec((tm, d), row),
        out_shape=jax.ShapeDtypeStruct((n, d), F32),
        compiler_params=_cparams(("parallel",)),
        name="ple",
    )(x2d, p2d, gpre, gpost, wg, wp)


def _rope_tables(seq):
    rows = seq // GRID_W
    row = np.repeat(np.arange(rows, dtype=np.float32), GRID_W)
    col = np.tile(np.arange(GRID_W, dtype=np.float32), rows)
    inv = (ROPE_THETA ** (-np.arange(0, AXIS_DIM, 2, dtype=np.float32) / AXIS_DIM)
           ).astype(np.float32)
    ang = np.concatenate([row[:, None] * inv, col[:, None] * inv], axis=-1)
    cos = np.repeat(np.cos(ang), 2, axis=1)
    sin = np.repeat(np.sin(ang), 2, axis=1)
    sign = np.tile(np.array([-1.0, 1.0], np.float32), HEAD_DIM // 2)
    cos_t = np.tile(cos, (1, N_KV_HEADS)).astype(np.float32)
    sin_t = np.tile(sin * sign, (1, N_KV_HEADS)).astype(np.float32)
    return jnp.asarray(cos_t), jnp.asarray(sin_t)


def _block_diag_mean(width):
    idx = np.arange(width) // HEAD_DIM
    return jnp.asarray((idx[:, None] == idx[None, :]).astype(np.float32) / HEAD_DIM,
                       dtype=BF16)


def _filter_features(seq, pad_to):
    t = np.linspace(0.0, 1.0, seq, dtype=np.float32)[:, None]
    w = ((2.0 * math.pi / seq) * np.arange(seq, dtype=np.float32)).astype(np.float32)
    bands = np.linspace(1e-4, FILT_BANDS - 1, FILT_BANDS, dtype=np.float32)
    ang = w[:, None] * bands[None, :]
    z = np.concatenate([t, np.cos(ang), -np.sin(ang)], axis=-1).astype(np.float32)
    return jnp.asarray(np.pad(z, ((0, 0), (0, pad_to - z.shape[1]))))


def _dft_tables(seq):
    m1 = seq // NP
    nf = 2 * m1
    nk = nf // 2 + 1
    kp = -(-nk // SUBLANES) * SUBLANES
    half_tile = LANES // 2
    kf = -(-nk // half_tile) * half_tile
    k = np.arange(nk, dtype=np.float64)[:, None]
    m = np.arange(m1, dtype=np.float64)[None, :]
    th = 2.0 * np.pi * k * m / nf
    fcs = np.zeros((2 * kf, m1), np.float64)
    fcs[:nk] = np.cos(th)
    fcs[kf:kf + nk] = -np.sin(th)
    ck = np.full((nk,), 2.0 / nf)
    ck[0] = 1.0 / nf
    ck[-1] = 1.0 / nf
    ginv = np.zeros((m1, 2 * kf), np.float64)
    ginv[:, :nk] = (np.cos(th) * ck[:, None]).T
    ginv[:, kf:kf + nk] = (-np.sin(th) * ck[:, None]).T
    th1 = 2.0 * np.pi * np.arange(kp, dtype=np.float64) / nf
    th1[nk:] = 0.0
    cos_k = np.repeat(np.cos(th1)[:, None], LANES, axis=1)
    sin_k = np.repeat(np.sin(th1)[:, None], LANES, axis=1)
    thq = th1[None, :] * np.arange(NP, dtype=np.float64)[:, None] / NP
    twc = np.repeat(np.cos(thq)[:, :, None], LANES, axis=2)
    tws = np.repeat(np.sin(thq)[:, :, None], LANES, axis=2)
    return tuple(jnp.asarray(t, dtype=F32) for t in (fcs, ginv, cos_k, sin_k, twc, tws))


def _pick_tile(n, want):
    t = min(n, want)
    while n % t:
        t //= 2
    return t


def kernel(x, p, ffn1_norm_pre, ffn1_norm_post, ffn1_w_gate, ffn1_w_up, ffn1_w_down,
           mix_norm_pre, mix_norm_post, w_in, hy_short_w, hy_short_b,
           filt_w1, filt_b1, filt_freq1, filt_w2, filt_b2, filt_freq2, filt_w3,
           filt_deltas, hy_bias, q_norm, k_norm, w_hy_out, w_att_out, w_out,
           ffn2_norm_pre, ffn2_norm_post, ffn2_w_gate, ffn2_w_up, ffn2_w_down,
           ple_norm_pre, ple_norm_post, w_ple_gate, w_ple_proj):
    batch, seq, d = x.shape
    depth = p.shape[0]
    n = batch * seq
    nhy = 3 * HY_WIDTH
    nmix = nhy + ATT_Q + 2 * ATT_KV
    hy_blocks = HY_WIDTH // LANES

    tm = _pick_tile(seq, 512)
    tq = _pick_tile(seq, 512)
    cos_t, sin_t = _rope_tables(seq)
    bdq = _block_diag_mean(ATT_Q)
    bdk = _block_diag_mean(ATT_KV)
    ze = LANES
    zf = _filter_features(seq, ze)
    row = lambda a: a.reshape(1, -1).astype(F32)
    bf = lambda a: a.astype(BF16)
    fcs, ginv, cos_k, sin_k, twc, tws = _dft_tables(seq)
    fcs, ginv = bf(fcs), bf(ginv)

    x2d = x.reshape(n, d)
    for i in range(depth):
        x2d = _ffn(x2d, row(ffn1_norm_pre[i]), row(ffn1_norm_post[i]),
                   bf(ffn1_w_gate[i]), bf(ffn1_w_up[i]), bf(ffn1_w_down[i]), tm)

        hy, q, k, v = _inproj(
            x2d, row(mix_norm_pre[i]), bf(w_in[i][:, :nmix]),
            row(jnp.tile(q_norm[i], N_Q_HEADS)), row(jnp.tile(k_norm[i], N_KV_HEADS)),
            bdq, bdk, cos_t, sin_t, seq, tm)

        w3 = filt_w3[i].reshape(FILT_HID, HY_ORDER, 2, HY_WIDTH)
        dl = filt_deltas[i]
        w1p = jnp.pad(filt_w1[i], ((0, ze - FILT_EMB), (0, 0)))
        hf, hb = _filt_mlp(
            zf, w1p, row(filt_b1[i]), row(filt_freq1[i]), filt_w2[i],
            row(filt_b2[i]), row(filt_freq2[i]),
            w3[:, :, 0].reshape(FILT_HID, -1), w3[:, :, 1].reshape(FILT_HID, -1),
            row(dl[:, 0]), row(dl[:, 1]), LANES)
        hr, hi = _filt_dft(hf, hb, fcs, cos_k, sin_k, twc, tws)

        hy3 = hy.reshape(batch, seq, nhy)
        sw = hy_short_w[i]
        sb = row(hy_short_b[i])
        z1 = _longconv(hy3, 0, hy3, hy_blocks, sw, sb, 0, row(hy_bias[i][0]),
                       0, hr, hi, twc, tws, fcs, ginv, True)
        ya = _longconv(z1, 0, hy3, 2 * hy_blocks, sw, sb, 0, row(hy_bias[i][1]),
                       hy_blocks, hr, hi, twc, tws, fcs, ginv, False)

        yb = _attention(q, k, v, batch, seq, tq)

        x2d = _merge(x2d, ya.reshape(n, HY_WIDTH), yb, row(mix_norm_pre[i]),
                     row(mix_norm_post[i]), bf(w_in[i][:, nmix:]), bf(w_hy_out[i]),
                     bf(w_att_out[i]), bf(w_out[i]), tm)

        x2d = _ffn(x2d, row(ffn2_norm_pre[i]), row(ffn2_norm_post[i]),
                   bf(ffn2_w_gate[i]), bf(ffn2_w_up[i]), bf(ffn2_w_down[i]), tm)

        x2d = _ple(x2d, p[i].reshape(n, -1), row(ple_norm_pre[i]),
                   row(ple_norm_post[i]), bf(w_ple_gate[i]), bf(w_ple_proj[i]), tm)
    return x2d.reshape(batch, seq, d)
```

```python
import functools
import math

import numpy as np
import jax
import jax.numpy as jnp
from jax import lax
from jax.experimental import pallas as pl
from jax.experimental.pallas import tpu as pltpu

F32 = jnp.float32
BF16 = jnp.bfloat16

GRID_W = 64
HY_WIDTH = 512
HY_ORDER = 2
FILT_EMB = 33
FILT_BANDS = (FILT_EMB - 1) // 2
FILT_HID = 64
N_Q_HEADS = 8
N_KV_HEADS = 2
GROUP = N_Q_HEADS // N_KV_HEADS
HEAD_DIM = 64
AXIS_DIM = HEAD_DIM // 2
ROPE_THETA = 10000.0
EPS = 1e-6
ATT_Q = N_Q_HEADS * HEAD_DIM
ATT_KV = N_KV_HEADS * HEAD_DIM

LANES = 128
SUBLANES = 8
VMEM_LIMIT_BYTES = 60 * 1024 * 1024

NP = 8


def _cparams(sem):
    return pltpu.CompilerParams(dimension_semantics=sem,
                                vmem_limit_bytes=VMEM_LIMIT_BYTES)


def _const_spec(shape):
    nd = len(shape)
    return pl.BlockSpec(shape, lambda *_: (0,) * nd, pipeline_mode=pl.Buffered(1))


def _rms(x, g):
    return x * lax.rsqrt(jnp.mean(x * x, axis=-1, keepdims=True) + EPS) * g


def _sigmoid(x):
    return 1.0 / (1.0 + jnp.exp(-x))


def _ffn_block(x, gpre_ref, gpost_ref, wg_ref, wu_ref, wd_ref):
    xn = _rms(x, gpre_ref[...]).astype(BF16)
    g = jnp.dot(xn, wg_ref[...], preferred_element_type=F32)
    u = jnp.dot(xn, wu_ref[...], preferred_element_type=F32)
    h = (g * _sigmoid(g) * u).astype(BF16)
    y = jnp.dot(h, wd_ref[...], preferred_element_type=F32)
    return x + 0.5 * _rms(y, gpost_ref[...])


def _ffn_kernel(x_ref, gpre_ref, gpost_ref, wg_ref, wu_ref, wd_ref, o_ref):
    o_ref[...] = _ffn_block(x_ref[...], gpre_ref, gpost_ref, wg_ref, wu_ref, wd_ref)


def _ffn(x2d, gpre, gpost, wg, wu, wd, tm):
    n, d = x2d.shape
    ff = wg.shape[1]
    return pl.pallas_call(
        _ffn_kernel,
        grid=(n // tm,),
        in_specs=[
            pl.BlockSpec((tm, d), lambda i: (i, 0)),
            _const_spec((1, d)), _const_spec((1, d)),
            _const_spec((d, ff)), _const_spec((d, ff)), _const_spec((ff, d)),
        ],
        out_specs=pl.BlockSpec((tm, d), lambda i: (i, 0)),
        out_shape=jax.ShapeDtypeStruct((n, d), F32),
        compiler_params=_cparams(("parallel",)),
        name="ffn",
    )(x2d, gpre, gpost, wg, wu, wd)


Q_SCALE = HEAD_DIM ** -0.5 * math.log2(math.e)


def _head_norm_rope(t, gain, bd, cos, sin):
    ms = jnp.dot((t * t).astype(BF16), bd, preferred_element_type=F32)
    tn = t * lax.rsqrt(ms + EPS) * gain
    w = t.shape[-1]
    lane = lax.broadcasted_iota(jnp.int32, tn.shape, 1)
    nxt = pltpu.roll(tn, w - 1, 1)
    prv = pltpu.roll(tn, 1, 1)
    swapped = jnp.where((lane & 1) == 0, nxt, prv)
    return tn * cos + swapped * sin


def _inproj_kernel(x_ref, gpre_ref, w_ref, qg_ref, kg_ref, bdq_ref, bdk_ref,
                   cos_ref, sin_ref, hy_ref, q_ref, k_ref, v_ref):
    nhy = hy_ref.shape[-1]
    u = _rms(x_ref[...], gpre_ref[...]).astype(BF16)
    proj = jnp.dot(u, w_ref[...], preferred_element_type=F32)
    hy_ref[...] = proj[:, :nhy]
    q = proj[:, nhy:nhy + ATT_Q]
    k = proj[:, nhy + ATT_Q:nhy + ATT_Q + ATT_KV]
    v = proj[:, nhy + ATT_Q + ATT_KV:]
    cos_k = cos_ref[...]
    sin_k = sin_ref[...]
    reps = ATT_Q // ATT_KV
    cos_q = jnp.concatenate([cos_k] * reps, axis=1)
    sin_q = jnp.concatenate([sin_k] * reps, axis=1)
    qr = _head_norm_rope(q, qg_ref[...], bdq_ref[...], cos_q, sin_q)
    kr = _head_norm_rope(k, kg_ref[...], bdk_ref[...], cos_k, sin_k)
    q_ref[...] = (qr * Q_SCALE).astype(BF16)
    k_ref[...] = kr.astype(BF16)
    v_ref[...] = v.astype(BF16)


def _inproj(x2d, gpre, w, qg, kg, bdq, bdk, cos_t, sin_t, seq, tm):
    n, d = x2d.shape
    ncols = w.shape[1]
    nhy = ncols - ATT_Q - 2 * ATT_KV
    tiles_per_seq = seq // tm
    return pl.pallas_call(
        _inproj_kernel,
        grid=(n // tm,),
        in_specs=[
            pl.BlockSpec((tm, d), lambda i: (i, 0)),
            _const_spec((1, d)),
            _const_spec((d, ncols)),
            _const_spec((1, ATT_Q)), _const_spec((1, ATT_KV)),
            _const_spec((ATT_Q, ATT_Q)), _const_spec((ATT_KV, ATT_KV)),
            pl.BlockSpec((tm, ATT_KV), lambda i: (i % tiles_per_seq, 0)),
            pl.BlockSpec((tm, ATT_KV), lambda i: (i % tiles_per_seq, 0)),
        ],
        out_specs=[
            pl.BlockSpec((tm, nhy), lambda i: (i, 0)),
            pl.BlockSpec((tm, ATT_Q), lambda i: (i, 0)),
            pl.BlockSpec((tm, ATT_KV), lambda i: (i, 0)),
            pl.BlockSpec((tm, ATT_KV), lambda i: (i, 0)),
        ],
        out_shape=[
            jax.ShapeDtypeStruct((n, nhy), F32),
            jax.ShapeDtypeStruct((n, ATT_Q), BF16),
            jax.ShapeDtypeStruct((n, ATT_KV), BF16),
            jax.ShapeDtypeStruct((n, ATT_KV), BF16),
        ],
        compiler_params=_cparams(("parallel",)),
        name="inproj",
    )(x2d, gpre, w, qg, kg, bdq, bdk, cos_t, sin_t)


def _attn_kernel(q_ref, k_ref, v_ref, o_ref, k0_ref, v0_ref):
    gw = GROUP * HEAD_DIM

    @pl.when(pl.program_id(1) == 0)
    def _():
        lane = lax.broadcasted_iota(jnp.int32, (k_ref.shape[0], gw), 1)
        reps = gw // ATT_KV
        kk = jnp.concatenate([k_ref[...].astype(F32)] * reps, axis=1)
        vv = jnp.concatenate([v_ref[...].astype(F32)] * reps, axis=1)
        for g in range(N_KV_HEADS):
            ks = kk if g == 0 else pltpu.roll(kk, gw - g * HEAD_DIM, 1)
            vs = vv if g == 0 else pltpu.roll(vv, gw - g * HEAD_DIM, 1)
            k0_ref[g] = jnp.where(lane < HEAD_DIM, ks, 0.0).astype(BF16)
            v0_ref[g] = jnp.where(lane < HEAD_DIM, vs,
                                  jnp.where(lane == HEAD_DIM, 1.0, 0.0)).astype(BF16)

    lane_o = lax.broadcasted_iota(jnp.int32, (q_ref.shape[0], gw), 1)
    for g in range(N_KV_HEADS):
        qg = q_ref[:, g * gw:(g + 1) * gw].astype(F32)
        acc = jnp.zeros((q_ref.shape[0], gw), F32)
        for h in range(GROUP):
            qh = qg if h == 0 else pltpu.roll(qg, gw - h * HEAD_DIM, 1)
            s = lax.dot_general(qh.astype(BF16), k0_ref[g], (((1,), (1,)), ((), ())),
                                preferred_element_type=F32)
            m = jnp.max(s, axis=-1, keepdims=True)
            p = jnp.exp2(s - m)
            o = jnp.dot(p.astype(BF16), v0_ref[g], preferred_element_type=F32)
            l = o[:, HEAD_DIM:HEAD_DIM + 1]
            o = jnp.where(lane_o < HEAD_DIM, o * (1.0 / l), 0.0)
            acc = acc + (o if h == 0 else pltpu.roll(o, h * HEAD_DIM, 1))
        o_ref[:, g * gw:(g + 1) * gw] = acc.astype(BF16)


def _attention(q, k, v, batch, seq, tq):
    n = q.shape[0]
    gw = GROUP * HEAD_DIM
    qt = seq // tq
    return pl.pallas_call(
        _attn_kernel,
        grid=(batch, qt),
        in_specs=[
            pl.BlockSpec((tq, ATT_Q), lambda b, i: (b * qt + i, 0)),
            pl.BlockSpec((seq, ATT_KV), lambda b, i: (b, 0)),
            pl.BlockSpec((seq, ATT_KV), lambda b, i: (b, 0)),
        ],
        out_specs=pl.BlockSpec((tq, ATT_Q), lambda b, i: (b * qt + i, 0)),
        out_shape=jax.ShapeDtypeStruct((n, ATT_Q), BF16),
        scratch_shapes=[pltpu.VMEM((N_KV_HEADS, seq, gw), BF16),
                        pltpu.VMEM((N_KV_HEADS, seq, gw), BF16)],
        compiler_params=_cparams(("arbitrary", "arbitrary")),
        name="attention",
    )(q, k, v)


def _filt_mlp_kernel(z_ref, w1_ref, b1_ref, f1_ref, w2_ref, b2_ref, f2_ref,
                     w3f_ref, w3b_ref, df_ref, db_ref, hf_ref, hb_ref, h_ref):
    hp = lax.Precision.HIGHEST

    @pl.when(pl.program_id(0) == 0)
    def _():
        h1 = jnp.sin(f1_ref[...] * (jnp.dot(z_ref[...], w1_ref[...], precision=hp,
                                            preferred_element_type=F32) + b1_ref[...]))
        h_ref[...] = jnp.sin(f2_ref[...] * (jnp.dot(h1, w2_ref[...], precision=hp,
                                                    preferred_element_type=F32)
                                            + b2_ref[...]))

    h = h_ref[...]
    t = z_ref[:, 0:1]
    hf = jnp.dot(h, w3f_ref[...], precision=hp, preferred_element_type=F32)
    hb = jnp.dot(h, w3b_ref[...], precision=hp, preferred_element_type=F32)
    hf = hf * jnp.exp(-t * jnp.abs(df_ref[...]))
    hb = hb * jnp.exp(-t * jnp.abs(db_ref[...]))
    row = lax.broadcasted_iota(jnp.int32, hf.shape, 0)
    both = jnp.abs(hf) + jnp.abs(hb)
    merged = jnp.abs(hf + hb)
    s = jnp.sum(jnp.where(row == 0, merged, both), axis=0, keepdims=True)
    inv = 1.0 / s
    hf_ref[...] = hf * inv
    hb_ref[...] = hb * inv


def _filt_mlp(zf, w1, b1, f1, w2, b2, f2, w3f, w3b, df, db, tc):
    seq, ze = zf.shape
    ncol = w3f.shape[1]
    hid = w2.shape[0]
    col = lambda j: (0, j)
    return pl.pallas_call(
        _filt_mlp_kernel,
        grid=(ncol // tc,),
        in_specs=[
            _const_spec((seq, ze)), _const_spec((ze, hid)), _const_spec((1, hid)),
            _const_spec((1, hid)), _const_spec((hid, hid)), _const_spec((1, hid)),
            _const_spec((1, hid)),
            pl.BlockSpec((hid, tc), col), pl.BlockSpec((hid, tc), col),
            pl.BlockSpec((1, tc), col), pl.BlockSpec((1, tc), col),
        ],
        out_specs=[pl.BlockSpec((seq, tc), col), pl.BlockSpec((seq, tc), col)],
        out_shape=[jax.ShapeDtypeStruct((seq, ncol), F32),
                   jax.ShapeDtypeStruct((seq, ncol), F32)],
        scratch_shapes=[pltpu.VMEM((seq, hid), F32)],
        compiler_params=_cparams(("arbitrary",)),
        name="filt_mlp",
    )(zf, w1, b1, f1, w2, b2, f2, w3f, w3b, df, db)


def _phases(ref, rows):
    return [ref[pl.ds(q, rows, stride=NP), :] for q in range(NP)]


def _split_bf16(x):
    hi = x.astype(BF16)
    lo = (x - hi.astype(F32)).astype(BF16)
    return hi, lo


def _cmul_tw(xr, xi, c, s, conj):
    if conj:
        return xr * c - xi * s, xi * c + xr * s
    return xr * c + xi * s, xi * c - xr * s


def _fft_list(xs, sign):
    n = len(xs)
    if n == 1:
        return xs
    even = _fft_list(xs[0::2], sign)
    odd = _fft_list(xs[1::2], sign)
    half = n // 2
    out = [None] * n
    for k in range(half):
        er, ei = even[k]
        o_r, o_i = odd[k]
        if k == 0:
            tr, ti = o_r, o_i
            out[k] = (er + tr, ei + ti)
            out[k + half] = (er - tr, ei - ti)
        elif 4 * k == n:
            if sign > 0:
                out[k] = (er - o_i, ei + o_r)
                out[k + half] = (er + o_i, ei - o_r)
            else:
                out[k] = (er + o_i, ei - o_r)
                out[k + half] = (er - o_i, ei + o_r)
        else:
            c = math.cos(2.0 * math.pi * k / n)
            s = sign * math.sin(2.0 * math.pi * k / n)
            tr = o_r * c - o_i * s
            ti = o_r * s + o_i * c
            out[k] = (er + tr, ei + ti)
            out[k + half] = (er - tr, ei - ti)
    return out


def _filt_dft_kernel(hf_ref, hb_ref, fcs_ref, cos_ref, sin_ref, twc_ref, tws_ref,
                     hr_ref, hi_ref):
    rows = hf_ref.shape[0] // NP
    kp = hr_ref.shape[1]
    kf = fcs_ref.shape[0] // 2
    xs = _phases(hf_ref, rows) + _phases(hb_ref, rows)
    xcat = jnp.concatenate(xs, axis=1)
    hi, lo = _split_bf16(xcat)
    fcs = fcs_ref[...]
    a = (jnp.dot(fcs, hi, preferred_element_type=F32)
         + jnp.dot(fcs, lo, preferred_element_type=F32))
    c = cos_ref[...]
    s = sin_ref[...]

    def part(dirn, ph):
        j = (dirn * NP + ph) * LANES
        return a[:kp, j:j + LANES], a[kf:kf + kp, j:j + LANES]

    fr0, fi0 = part(0, 0)
    br0, bi0 = part(1, 0)
    gs = [(fr0 + br0, fi0 - bi0)]
    for d in range(1, NP):
        fr, fi = part(0, d)
        br, bi = part(1, NP - d)
        gr = fr + c * br + s * bi
        gi = fi + s * br - c * bi
        gs.append(_cmul_tw(gr, gi, twc_ref[d], tws_ref[d], False))
    hs = _fft_list(gs, -1)
    for k2 in range(NP):
        hr_ref[k2] = hs[k2][0] * (1.0 / NP)
        hi_ref[k2] = hs[k2][1] * (1.0 / NP)


def _filt_dft(hf, hb, fcs, cos_k, sin_k, twc, tws):
    seq, ncol = hf.shape
    kp = cos_k.shape[0]
    col = lambda j: (0, j)
    out_spec = pl.BlockSpec((NP, kp, LANES), lambda j: (0, 0, j))
    return pl.pallas_call(
        _filt_dft_kernel,
        grid=(ncol // LANES,),
        in_specs=[pl.BlockSpec((seq, LANES), col), pl.BlockSpec((seq, LANES), col),
                  _const_spec(fcs.shape), _const_spec(cos_k.shape),
                  _const_spec(sin_k.shape), _const_spec(twc.shape),
                  _const_spec(tws.shape)],
        out_specs=[out_spec, out_spec],
        out_shape=[jax.ShapeDtypeStruct((NP, kp, ncol), F32),
                   jax.ShapeDtypeStruct((NP, kp, ncol), F32)],
        compiler_params=_cparams(("parallel",)),
        name="filt_dft",
    )(hf, hb, fcs, cos_k, sin_k, twc, tws)


def _short_conv_phases(raw, w_ref, b_ref):
    rows = raw[0].shape[0]
    row = lax.broadcasted_iota(jnp.int32, raw[0].shape, 0)
    before = jnp.where(row == 0, 0.0, pltpu.roll(raw[NP - 1], 1, 0))
    after = jnp.where(row == rows - 1, 0.0, pltpu.roll(raw[0], rows - 1, 0))
    w0 = w_ref[0:1, :]
    w1 = w_ref[1:2, :]
    w2 = w_ref[2:3, :]
    b = b_ref[...]
    out = []
    for q in range(NP):
        prev = raw[q - 1] if q > 0 else before
        nxt = raw[q + 1] if q < NP - 1 else after
        out.append(prev * w0 + raw[q] * w1 + nxt * w2 + b)
    return out


def _longconv_kernel(x_ref, g_ref, wx_ref, bx_ref, wg_ref, bg_ref, bias_ref,
                     hr_ref, hi_ref, twc_ref, tws_ref, fcs_ref, ginv_ref, o_ref,
                     a_ref, b_ref, *, conv_x):
    rows = x_ref.shape[0] // NP
    kp = hr_ref.shape[1]
    kf = fcs_ref.shape[0] // 2
    xs = _phases(x_ref, rows)
    if conv_x:
        xs = _short_conv_phases(xs, wx_ref, bx_ref)
    gs = _short_conv_phases(_phases(g_ref, rows), wg_ref, bg_ref)

    xcat = jnp.concatenate([x.astype(BF16) for x in xs], axis=1)
    a_ref[...] = jnp.dot(fcs_ref[...], xcat, preferred_element_type=F32)

    def body(i):
        r0 = i * SUBLANES
        rs = pl.ds(r0, SUBLANES)
        zs = []
        for q in range(NP):
            ar = a_ref[rs, q * LANES:(q + 1) * LANES]
            ai = a_ref[pl.ds(kf + r0, SUBLANES), q * LANES:(q + 1) * LANES]
            zs.append((ar, ai) if q == 0 else
                      _cmul_tw(ar, ai, twc_ref[q, rs, :], tws_ref[q, rs, :], False))
        zs = _fft_list(zs, -1)
        ys = []
        for k2 in range(NP):
            zr, zi = zs[k2]
            hr = hr_ref[k2, rs, :]
            hi = hi_ref[k2, rs, :]
            ys.append((hr * zr - hi * zi, hr * zi + hi * zr))
        ys = _fft_list(ys, 1)
        for p in range(NP):
            br, bi = ys[p]
            if p > 0:
                br, bi = _cmul_tw(br, bi, twc_ref[p, rs, :], tws_ref[p, rs, :], True)
            b_ref[rs, p * LANES:(p + 1) * LANES] = br
            b_ref[pl.ds(kf + r0, SUBLANES), p * LANES:(p + 1) * LANES] = bi

    for i in range(kp // SUBLANES):
        body(i)
    if kf > kp:
        zeros = jnp.zeros((kf - kp, NP * LANES), F32)
        b_ref[kp:kf, :] = zeros
        b_ref[kf + kp:, :] = zeros

    y = jnp.dot(ginv_ref[...], b_ref[...].astype(BF16), preferred_element_type=F32)
    bias = bias_ref[...]
    for p in range(NP):
        yp = y[:, p * LANES:(p + 1) * LANES]
        o_ref[pl.ds(p, rows, stride=NP), :] = gs[p] * (yp + xs[p] * bias)


def _longconv(x3, x_blk0, hy3, g_blk0, sw, sb, wx_blk0, bias, h_blk0, hr, hi,
              twc, tws, fcs, ginv, conv_x):
    batch, seq, _ = hy3.shape
    nch = bias.shape[1] // LANES
    kp = hr.shape[1]
    kernel = functools.partial(_longconv_kernel, conv_x=conv_x)
    return pl.pallas_call(
        kernel,
        grid=(nch, batch),
        in_specs=[
            pl.BlockSpec((None, seq, LANES), lambda c, b: (b, 0, x_blk0 + c)),
            pl.BlockSpec((None, seq, LANES), lambda c, b: (b, 0, g_blk0 + c)),
            pl.BlockSpec((3, LANES), lambda c, b: (0, wx_blk0 + c)),
            pl.BlockSpec((1, LANES), lambda c, b: (0, wx_blk0 + c)),
            pl.BlockSpec((3, LANES), lambda c, b: (0, g_blk0 + c)),
            pl.BlockSpec((1, LANES), lambda c, b: (0, g_blk0 + c)),
            pl.BlockSpec((1, LANES), lambda c, b: (0, c)),
            pl.BlockSpec((NP, kp, LANES), lambda c, b: (0, 0, h_blk0 + c),
                         pipeline_mode=pl.Buffered(1)),
            pl.BlockSpec((NP, kp, LANES), lambda c, b: (0, 0, h_blk0 + c),
                         pipeline_mode=pl.Buffered(1)),
            _const_spec(twc.shape), _const_spec(tws.shape),
            _const_spec(fcs.shape), _const_spec(ginv.shape),
        ],
        out_specs=pl.BlockSpec((None, seq, LANES), lambda c, b: (b, 0, c)),
        out_shape=jax.ShapeDtypeStruct((batch, seq, nch * LANES), F32),
        scratch_shapes=[pltpu.VMEM((fcs.shape[0], NP * LANES), F32),
                        pltpu.VMEM((fcs.shape[0], NP * LANES), F32)],
        compiler_params=_cparams(("arbitrary", "arbitrary")),
        name="longconv_x" if conv_x else "longconv",
    )(x3, hy3, sw, sb, sw, sb, bias, hr, hi, twc, tws, fcs, ginv)


def _merge_kernel(x_ref, ya_ref, yb_ref, gpre_ref, gpost_ref, wgate_ref,
                  wa_ref, wb_ref, wo_ref, o_ref):
    x = x_ref[...]
    d = x.shape[-1]
    u = _rms(x, gpre_ref[...]).astype(BF16)
    gates = jnp.dot(u, wgate_ref[...], preferred_element_type=F32)
    ma = jnp.dot(ya_ref[...].astype(BF16), wa_ref[...], preferred_element_type=F32)
    mb = jnp.dot(yb_ref[...], wb_ref[...], preferred_element_type=F32)
    m = _sigmoid(gates[:, :d]) * ma + _sigmoid(gates[:, d:]) * mb
    y = jnp.dot(m.astype(BF16), wo_ref[...], preferred_element_type=F32)
    o_ref[...] = x + _rms(y, gpost_ref[...])


def _merge(x2d, ya, yb, gpre, gpost, wgate, wa, wb, wo, tm):
    n, d = x2d.shape
    row = lambda i: (i, 0)
    return pl.pallas_call(
        _merge_kernel,
        grid=(n // tm,),
        in_specs=[
            pl.BlockSpec((tm, d), row),
            pl.BlockSpec((tm, ya.shape[1]), row),
            pl.BlockSpec((tm, yb.shape[1]), row),
            _const_spec((1, d)), _const_spec((1, d)),
            _const_spec(wgate.shape), _const_spec(wa.shape),
            _const_spec(wb.shape), _const_spec(wo.shape),
        ],
        out_specs=pl.BlockSpec((tm, d), row),
        out_shape=jax.ShapeDtypeStruct((n, d), F32),
        compiler_params=_cparams(("parallel",)),
        name="merge",
    )(x2d, ya, yb, gpre, gpost, wgate, wa, wb, wo)


def _ffn_ple_kernel(x_ref, gpre_ref, gpost_ref, wg_ref, wu_ref, wd_ref,
                    p_ref, ppre_ref, ppost_ref, wpg_ref, wpp_ref, o_ref):
    x1 = _ffn_block(x_ref[...], gpre_ref, gpost_ref, wg_ref, wu_ref, wd_ref)
    u = _rms(x1, ppre_ref[...]).astype(BF16)
    g = _sigmoid(jnp.dot(u, wpg_ref[...], preferred_element_type=F32))
    e = jnp.dot(p_ref[...].astype(BF16), wpp_ref[...], preferred_element_type=F32)
    o_ref[...] = x1 + _rms(g * e, ppost_ref[...])


def _ffn_ple(x2d, gpre, gpost, wg, wu, wd, p2d, ppre, ppost, wpg, wpp, tm):
    n, d = x2d.shape
    row = lambda i: (i, 0)
    return pl.pallas_call(
        _ffn_ple_kernel,
        grid=(n // tm,),
        in_specs=[
            pl.BlockSpec((tm, d), row),
            _const_spec((1, d)), _const_spec((1, d)),
            _const_spec(wg.shape), _const_spec(wu.shape), _const_spec(wd.shape),
            pl.BlockSpec((tm, p2d.shape[1]), row),
            _const_spec((1, d)), _const_spec((1, d)),
            _const_spec(wpg.shape), _const_spec(wpp.shape),
        ],
        out_specs=pl.BlockSpec((tm, d), row),
        out_shape=jax.ShapeDtypeStruct((n, d), F32),
        compiler_params=_cparams(("parallel",)),
        name="ffn_ple",
    )(x2d, gpre, gpost, wg, wu, wd, p2d, ppre, ppost, wpg, wpp)


def _rope_tables(seq):
    rows = seq // GRID_W
    row = np.repeat(np.arange(rows, dtype=np.float32), GRID_W)
    col = np.tile(np.arange(GRID_W, dtype=np.float32), rows)
    inv = (ROPE_THETA ** (-np.arange(0, AXIS_DIM, 2, dtype=np.float32) / AXIS_DIM)
           ).astype(np.float32)
    ang = np.concatenate([row[:, None] * inv, col[:, None] * inv], axis=-1)
    cos = np.repeat(np.cos(ang), 2, axis=1)
    sin = np.repeat(np.sin(ang), 2, axis=1)
    sign = np.tile(np.array([-1.0, 1.0], np.float32), HEAD_DIM // 2)
    cos_t = np.tile(cos, (1, N_KV_HEADS)).astype(np.float32)
    sin_t = np.tile(sin * sign, (1, N_KV_HEADS)).astype(np.float32)
    return jnp.asarray(cos_t), jnp.asarray(sin_t)


def _block_diag_mean(width):
    idx = np.arange(width) // HEAD_DIM
    return jnp.asarray((idx[:, None] == idx[None, :]).astype(np.float32) / HEAD_DIM,
                       dtype=BF16)


def _filter_features(seq, pad_to):
    t = np.linspace(0.0, 1.0, seq, dtype=np.float32)[:, None]
    w = ((2.0 * math.pi / seq) * np.arange(seq, dtype=np.float32)).astype(np.float32)
    bands = np.linspace(1e-4, FILT_BANDS - 1, FILT_BANDS, dtype=np.float32)
    ang = w[:, None] * bands[None, :]
    z = np.concatenate([t, np.cos(ang), -np.sin(ang)], axis=-1).astype(np.float32)
    return jnp.asarray(np.pad(z, ((0, 0), (0, pad_to - z.shape[1]))))


def _dft_tables(seq):
    m1 = seq // NP
    nf = 2 * m1
    nk = nf // 2 + 1
    kp = -(-nk // SUBLANES) * SUBLANES
    half_tile = LANES // 2
    kf = -(-nk // half_tile) * half_tile
    k = np.arange(nk, dtype=np.float64)[:, None]
    m = np.arange(m1, dtype=np.float64)[None, :]
    th = 2.0 * np.pi * k * m / nf
    fcs = np.zeros((2 * kf, m1), np.float64)
    fcs[:nk] = np.cos(th)
    fcs[kf:kf + nk] = -np.sin(th)
    ck = np.full((nk,), 2.0 / nf)
    ck[0] = 1.0 / nf
    ck[-1] = 1.0 / nf
    ginv = np.zeros((m1, 2 * kf), np.float64)
    ginv[:, :nk] = (np.cos(th) * ck[:, None]).T
    ginv[:, kf:kf + nk] = (-np.sin(th) * ck[:, None]).T
    th1 = 2.0 * np.pi * np.arange(kp, dtype=np.float64) / nf
    th1[nk:] = 0.0
    cos_k = np.repeat(np.cos(th1)[:, None], LANES, axis=1)
    sin_k = np.repeat(np.sin(th1)[:, None], LANES, axis=1)
    thq = th1[None, :] * np.arange(NP, dtype=np.float64)[:, None] / NP
    twc = np.repeat(np.cos(thq)[:, :, None], LANES, axis=2)
    tws = np.repeat(np.sin(thq)[:, :, None], LANES, axis=2)
    return tuple(jnp.asarray(t, dtype=F32) for t in (fcs, ginv, cos_k, sin_k, twc, tws))


def _pick_tile(n, want):
    t = min(n, want)
    while n % t:
        t //= 2
    return t


def kernel(x, p, ffn1_norm_pre, ffn1_norm_post, ffn1_w_gate, ffn1_w_up, ffn1_w_down,
           mix_norm_pre, mix_norm_post, w_in, hy_short_w, hy_short_b,
           filt_w1, filt_b1, filt_freq1, filt_w2, filt_b2, filt_freq2, filt_w3,
           filt_deltas, hy_bias, q_norm, k_norm, w_hy_out, w_att_out, w_out,
           ffn2_norm_pre, ffn2_norm_post, ffn2_w_gate, ffn2_w_up, ffn2_w_down,
           ple_norm_pre, ple_norm_post, w_ple_gate, w_ple_proj):
    batch, seq, d = x.shape
    depth = p.shape[0]
    n = batch * seq
    nhy = 3 * HY_WIDTH
    nmix = nhy + ATT_Q + 2 * ATT_KV
    hy_blocks = HY_WIDTH // LANES

    tm = _pick_tile(seq, 512)
    tq = _pick_tile(seq, 512)
    cos_t, sin_t = _rope_tables(seq)
    bdq = _block_diag_mean(ATT_Q)
    bdk = _block_diag_mean(ATT_KV)
    ze = LANES
    zf = _filter_features(seq, ze)
    row = lambda a: a.reshape(1, -1).astype(F32)
    bf = lambda a: a.astype(BF16)
    fcs, ginv, cos_k, sin_k, twc, tws = _dft_tables(seq)
    fcs, ginv = bf(fcs), bf(ginv)

    x2d = x.reshape(n, d)
    for i in range(depth):
        x2d = _ffn(x2d, row(ffn1_norm_pre[i]), row(ffn1_norm_post[i]),
                   bf(ffn1_w_gate[i]), bf(ffn1_w_up[i]), bf(ffn1_w_down[i]), tm)

        hy, q, k, v = _inproj(
            x2d, row(mix_norm_pre[i]), bf(w_in[i][:, :nmix]),
            row(jnp.tile(q_norm[i], N_Q_HEADS)), row(jnp.tile(k_norm[i], N_KV_HEADS)),
            bdq, bdk, cos_t, sin_t, seq, tm)

        w3 = filt_w3[i].reshape(FILT_HID, HY_ORDER, 2, HY_WIDTH)
        dl = filt_deltas[i]
        w1p = jnp.pad(filt_w1[i], ((0, ze - FILT_EMB), (0, 0)))
        hf, hb = _filt_mlp(
            zf, w1p, row(filt_b1[i]), row(filt_freq1[i]), filt_w2[i],
            row(filt_b2[i]), row(filt_freq2[i]),
            w3[:, :, 0].reshape(FILT_HID, -1), w3[:, :, 1].reshape(FILT_HID, -1),
            row(dl[:, 0]), row(dl[:, 1]), LANES)
        hr, hi = _filt_dft(hf, hb, fcs, cos_k, sin_k, twc, tws)

        hy3 = hy.reshape(batch, seq, nhy)
        sw = hy_short_w[i]
        sb = row(hy_short_b[i])
        z1 = _longconv(hy3, 0, hy3, hy_blocks, sw, sb, 0, row(hy_bias[i][0]),
                       0, hr, hi, twc, tws, fcs, ginv, True)
        ya = _longconv(z1, 0, hy3, 2 * hy_blocks, sw, sb, 0, row(hy_bias[i][1]),
                       hy_blocks, hr, hi, twc, tws, fcs, ginv, False)

        yb = _attention(q, k, v, batch, seq, tq)

        x2d = _merge(x2d, ya.reshape(n, HY_WIDTH), yb, row(mix_norm_pre[i]),
                     row(mix_norm_post[i]), bf(w_in[i][:, nmix:]), bf(w_hy_out[i]),
                     bf(w_att_out[i]), bf(w_out[i]), tm)

        x2d = _ffn_ple(x2d, row(ffn2_norm_pre[i]), row(ffn2_norm_post[i]),
                       bf(ffn2_w_gate[i]), bf(ffn2_w_up[i]), bf(ffn2_w_down[i]),
                       p[i].reshape(n, -1), row(ple_norm_pre[i]), row(ple_norm_post[i]),
                       bf(w_ple_gate[i]), bf(w_ple_proj[i]), tm)
    return x2d.reshape(batch, seq, d)
```

```python
import functools
import math

import numpy as np
import jax
import jax.numpy as jnp
from jax import lax
from jax.experimental import pallas as pl
from jax.experimental.pallas import tpu as pltpu

F32 = jnp.float32
BF16 = jnp.bfloat16

GRID_W = 64
HY_WIDTH = 512
HY_ORDER = 2
FILT_EMB = 33
FILT_BANDS = (FILT_EMB - 1) // 2
FILT_HID = 64
N_Q_HEADS = 8
N_KV_HEADS = 2
GROUP = N_Q_HEADS // N_KV_HEADS
HEAD_DIM = 64
AXIS_DIM = HEAD_DIM // 2
ROPE_THETA = 10000.0
EPS = 1e-6
ATT_Q = N_Q_HEADS * HEAD_DIM
ATT_KV = N_KV_HEADS * HEAD_DIM

LANES = 128
SUBLANES = 8
VMEM_LIMIT_BYTES = 60 * 1024 * 1024

NP = 8


def _cparams(sem):
    return pltpu.CompilerParams(dimension_semantics=sem,
                                vmem_limit_bytes=VMEM_LIMIT_BYTES)


def _const_spec(shape):
    nd = len(shape)
    return pl.BlockSpec(shape, lambda *_: (0,) * nd, pipeline_mode=pl.Buffered(1))


def _rms(x, g):
    return x * lax.rsqrt(jnp.mean(x * x, axis=-1, keepdims=True) + EPS) * g


def _sigmoid(x):
    return 1.0 / (1.0 + jnp.exp(-x))


def _ffn_block(x, gpre_ref, gpost_ref, wg_ref, wu_ref, wd_ref):
    xn = _rms(x, gpre_ref[...]).astype(BF16)
    g = jnp.dot(xn, wg_ref[...], preferred_element_type=F32)
    u = jnp.dot(xn, wu_ref[...], preferred_element_type=F32)
    h = (g * _sigmoid(g) * u).astype(BF16)
    y = jnp.dot(h, wd_ref[...], preferred_element_type=F32)
    return x + 0.5 * _rms(y, gpost_ref[...])


def _ffn_kernel(x_ref, gpre_ref, gpost_ref, wg_ref, wu_ref, wd_ref, o_ref):
    o_ref[...] = _ffn_block(x_ref[...], gpre_ref, gpost_ref, wg_ref, wu_ref, wd_ref)


def _ffn(x2d, gpre, gpost, wg, wu, wd, tm):
    n, d = x2d.shape
    ff = wg.shape[1]
    return pl.pallas_call(
        _ffn_kernel,
        grid=(n // tm,),
        in_specs=[
            pl.BlockSpec((tm, d), lambda i: (i, 0)),
            _const_spec((1, d)), _const_spec((1, d)),
            _const_spec((d, ff)), _const_spec((d, ff)), _const_spec((ff, d)),
        ],
        out_specs=pl.BlockSpec((tm, d), lambda i: (i, 0)),
        out_shape=jax.ShapeDtypeStruct((n, d), F32),
        compiler_params=_cparams(("parallel",)),
        name="ffn",
    )(x2d, gpre, gpost, wg, wu, wd)


Q_SCALE = HEAD_DIM ** -0.5 * math.log2(math.e)


def _head_norm_rope(t, gain, bd, cos, sin):
    ms = jnp.dot((t * t).astype(BF16), bd, preferred_element_type=F32)
    tn = t * lax.rsqrt(ms + EPS) * gain
    w = t.shape[-1]
    lane = lax.broadcasted_iota(jnp.int32, tn.shape, 1)
    nxt = pltpu.roll(tn, w - 1, 1)
    prv = pltpu.roll(tn, 1, 1)
    swapped = jnp.where((lane & 1) == 0, nxt, prv)
    return tn * cos + swapped * sin


def _inproj_kernel(x_ref, gpre_ref, w_ref, qg_ref, kg_ref, bdq_ref, bdk_ref,
                   cos_ref, sin_ref, hy_ref, q_ref, k_ref, v_ref):
    nhy = hy_ref.shape[-1]
    u = _rms(x_ref[...], gpre_ref[...]).astype(BF16)
    proj = jnp.dot(u, w_ref[...], preferred_element_type=F32)
    hy_ref[...] = proj[:, :nhy]
    q = proj[:, nhy:nhy + ATT_Q]
    k = proj[:, nhy + ATT_Q:nhy + ATT_Q + ATT_KV]
    v = proj[:, nhy + ATT_Q + ATT_KV:]
    cos_k = cos_ref[...]
    sin_k = sin_ref[...]
    reps = ATT_Q // ATT_KV
    cos_q = jnp.concatenate([cos_k] * reps, axis=1)
    sin_q = jnp.concatenate([sin_k] * reps, axis=1)
    qr = _head_norm_rope(q, qg_ref[...], bdq_ref[...], cos_q, sin_q)
    kr = _head_norm_rope(k, kg_ref[...], bdk_ref[...], cos_k, sin_k)
    q_ref[...] = (qr * Q_SCALE).astype(BF16)
    k_ref[...] = kr.astype(BF16)
    v_ref[...] = v.astype(BF16)


def _inproj(x2d, gpre, w, qg, kg, bdq, bdk, cos_t, sin_t, seq, tm):
    n, d = x2d.shape
    ncols = w.shape[1]
    nhy = ncols - ATT_Q - 2 * ATT_KV
    tiles_per_seq = seq // tm
    return pl.pallas_call(
        _inproj_kernel,
        grid=(n // tm,),
        in_specs=[
            pl.BlockSpec((tm, d), lambda i: (i, 0)),
            _const_spec((1, d)),
            _const_spec((d, ncols)),
            _const_spec((1, ATT_Q)), _const_spec((1, ATT_KV)),
            _const_spec((ATT_Q, ATT_Q)), _const_spec((ATT_KV, ATT_KV)),
            pl.BlockSpec((tm, ATT_KV), lambda i: (i % tiles_per_seq, 0)),
            pl.BlockSpec((tm, ATT_KV), lambda i: (i % tiles_per_seq, 0)),
        ],
        out_specs=[
            pl.BlockSpec((tm, nhy), lambda i: (i, 0)),
            pl.BlockSpec((tm, ATT_Q), lambda i: (i, 0)),
            pl.BlockSpec((tm, ATT_KV), lambda i: (i, 0)),
            pl.BlockSpec((tm, ATT_KV), lambda i: (i, 0)),
        ],
        out_shape=[
            jax.ShapeDtypeStruct((n, nhy), F32),
            jax.ShapeDtypeStruct((n, ATT_Q), BF16),
            jax.ShapeDtypeStruct((n, ATT_KV), BF16),
            jax.ShapeDtypeStruct((n, ATT_KV), BF16),
        ],
        compiler_params=_cparams(("parallel",)),
        name="inproj",
    )(x2d, gpre, w, qg, kg, bdq, bdk, cos_t, sin_t)


SHIFT_BOUND_MAX = 60.0
BOUND_SLACK = 1.02


def _attn_kernel(q_ref, k_ref, v_ref, bdq_ref, bdk_ref, o_ref, k0_ref, v0_ref, kn_ref):
    gw = GROUP * HEAD_DIM
    tq = q_ref.shape[0]

    @pl.when(pl.program_id(1) == 0)
    def _():
        lane = lax.broadcasted_iota(jnp.int32, (k_ref.shape[0], gw), 1)
        reps = gw // ATT_KV
        kf = k_ref[...].astype(F32)
        kk = jnp.concatenate([kf] * reps, axis=1)
        vv = jnp.concatenate([v_ref[...].astype(F32)] * reps, axis=1)
        for g in range(N_KV_HEADS):
            ks = kk if g == 0 else pltpu.roll(kk, gw - g * HEAD_DIM, 1)
            vs = vv if g == 0 else pltpu.roll(vv, gw - g * HEAD_DIM, 1)
            k0_ref[g] = jnp.where(lane < HEAD_DIM, ks, 0.0).astype(BF16)
            v0_ref[g] = jnp.where(lane < HEAD_DIM, vs,
                                  jnp.where(lane == HEAD_DIM, 1.0, 0.0)).astype(BF16)
        kms = jnp.dot((kf * kf).astype(BF16), bdk_ref[...], preferred_element_type=F32)
        kn_ref[...] = jnp.full(kn_ref.shape, jnp.max(kms), F32)

    qf = q_ref[...].astype(F32)
    qms = jnp.dot((qf * qf).astype(BF16), bdq_ref[...], preferred_element_type=F32)
    bound = (HEAD_DIM * BOUND_SLACK) * jnp.sqrt(qms * kn_ref[0:1, 0:1])
    bound_ok = jnp.max(bound) < SHIFT_BOUND_MAX
    lane_o = lax.broadcasted_iota(jnp.int32, (tq, gw), 1)

    def heads(use_bound):
        for g in range(N_KV_HEADS):
            qg = qf[:, g * gw:(g + 1) * gw]
            acc = jnp.zeros((tq, gw), F32)
            for h in range(GROUP):
                qh = qg if h == 0 else pltpu.roll(qg, gw - h * HEAD_DIM, 1)
                s = lax.dot_general(qh.astype(BF16), k0_ref[g], (((1,), (1,)), ((), ())),
                                    preferred_element_type=F32)
                if use_bound:
                    col = (g * GROUP + h) * HEAD_DIM
                    m = bound[:, col:col + 1]
                else:
                    m = jnp.max(s, axis=-1, keepdims=True)
                p = jnp.exp2(s - m)
                o = jnp.dot(p.astype(BF16), v0_ref[g], preferred_element_type=F32)
                l = o[:, HEAD_DIM:HEAD_DIM + 1]
                o = jnp.where(lane_o < HEAD_DIM, o * (1.0 / l), 0.0)
                acc = acc + (o if h == 0 else pltpu.roll(o, h * HEAD_DIM, 1))
            o_ref[:, g * gw:(g + 1) * gw] = acc.astype(BF16)

    @pl.when(bound_ok)
    def _():
        heads(True)

    @pl.when(jnp.logical_not(bound_ok))
    def _():
        heads(False)


def _attention(q, k, v, bdq, bdk, batch, seq, tq):
    n = q.shape[0]
    gw = GROUP * HEAD_DIM
    qt = seq // tq
    return pl.pallas_call(
        _attn_kernel,
        grid=(batch, qt),
        in_specs=[
            pl.BlockSpec((tq, ATT_Q), lambda b, i: (b * qt + i, 0)),
            pl.BlockSpec((seq, ATT_KV), lambda b, i: (b, 0)),
            pl.BlockSpec((seq, ATT_KV), lambda b, i: (b, 0)),
            _const_spec(bdq.shape), _const_spec(bdk.shape),
        ],
        out_specs=pl.BlockSpec((tq, ATT_Q), lambda b, i: (b * qt + i, 0)),
        out_shape=jax.ShapeDtypeStruct((n, ATT_Q), BF16),
        scratch_shapes=[pltpu.VMEM((N_KV_HEADS, seq, gw), BF16),
                        pltpu.VMEM((N_KV_HEADS, seq, gw), BF16),
                        pltpu.VMEM((SUBLANES, LANES), F32)],
        compiler_params=_cparams(("arbitrary", "arbitrary")),
        name="attention",
    )(q, k, v, bdq, bdk)


def _filt_mlp_kernel(z_ref, w1_ref, b1_ref, f1_ref, w2_ref, b2_ref, f2_ref,
                     w3f_ref, w3b_ref, df_ref, db_ref, hf_ref, hb_ref, h_ref):
    hp = lax.Precision.HIGHEST

    @pl.when(pl.program_id(0) == 0)
    def _():
        h1 = jnp.sin(f1_ref[...] * (jnp.dot(z_ref[...], w1_ref[...], precision=hp,
                                            preferred_element_type=F32) + b1_ref[...]))
        h_ref[...] = jnp.sin(f2_ref[...] * (jnp.dot(h1, w2_ref[...], precision=hp,
                                                    preferred_element_type=F32)
                                            + b2_ref[...]))

    h = h_ref[...]
    t = z_ref[:, 0:1]
    hf = jnp.dot(h, w3f_ref[...], precision=hp, preferred_element_type=F32)
    hb = jnp.dot(h, w3b_ref[...], precision=hp, preferred_element_type=F32)
    hf = hf * jnp.exp(-t * jnp.abs(df_ref[...]))
    hb = hb * jnp.exp(-t * jnp.abs(db_ref[...]))
    row = lax.broadcasted_iota(jnp.int32, hf.shape, 0)
    both = jnp.abs(hf) + jnp.abs(hb)
    merged = jnp.abs(hf + hb)
    s = jnp.sum(jnp.where(row == 0, merged, both), axis=0, keepdims=True)
    inv = 1.0 / s
    hf_ref[...] = hf * inv
    hb_ref[...] = hb * inv


def _filt_mlp(zf, w1, b1, f1, w2, b2, f2, w3f, w3b, df, db, tc):
    seq, ze = zf.shape
    ncol = w3f.shape[1]
    hid = w2.shape[0]
    col = lambda j: (0, j)
    return pl.pallas_call(
        _filt_mlp_kernel,
        grid=(ncol // tc,),
        in_specs=[
            _const_spec((seq, ze)), _const_spec((ze, hid)), _const_spec((1, hid)),
            _const_spec((1, hid)), _const_spec((hid, hid)), _const_spec((1, hid)),
            _const_spec((1, hid)),
            pl.BlockSpec((hid, tc), col), pl.BlockSpec((hid, tc), col),
            pl.BlockSpec((1, tc), col), pl.BlockSpec((1, tc), col),
        ],
        out_specs=[pl.BlockSpec((seq, tc), col), pl.BlockSpec((seq, tc), col)],
        out_shape=[jax.ShapeDtypeStruct((seq, ncol), F32),
                   jax.ShapeDtypeStruct((seq, ncol), F32)],
        scratch_shapes=[pltpu.VMEM((seq, hid), F32)],
        compiler_params=_cparams(("arbitrary",)),
        name="filt_mlp",
    )(zf, w1, b1, f1, w2, b2, f2, w3f, w3b, df, db)


def _phases(ref, rows):
    return [ref[pl.ds(q, rows, stride=NP), :] for q in range(NP)]


def _split_bf16(x):
    hi = x.astype(BF16)
    lo = (x - hi.astype(F32)).astype(BF16)
    return hi, lo


def _cmul_tw(xr, xi, c, s, conj):
    if conj:
        return xr * c - xi * s, xi * c + xr * s
    return xr * c + xi * s, xi * c - xr * s


def _fft_list(xs, sign):
    n = len(xs)
    if n == 1:
        return xs
    even = _fft_list(xs[0::2], sign)
    odd = _fft_list(xs[1::2], sign)
    half = n // 2
    out = [None] * n
    for k in range(half):
        er, ei = even[k]
        o_r, o_i = odd[k]
        if k == 0:
            tr, ti = o_r, o_i
            out[k] = (er + tr, ei + ti)
            out[k + half] = (er - tr, ei - ti)
        elif 4 * k == n:
            if sign > 0:
                out[k] = (er - o_i, ei + o_r)
                out[k + half] = (er + o_i, ei - o_r)
            else:
                out[k] = (er + o_i, ei - o_r)
                out[k + half] = (er - o_i, ei + o_r)
        else:
            c = math.cos(2.0 * math.pi * k / n)
            s = sign * math.sin(2.0 * math.pi * k / n)
            tr = o_r * c - o_i * s
            ti = o_r * s + o_i * c
            out[k] = (er + tr, ei + ti)
            out[k + half] = (er - tr, ei - ti)
    return out


def _filt_dft_kernel(hf_ref, hb_ref, fcs_ref, cos_ref, sin_ref, twc_ref, tws_ref,
                     hr_ref, hi_ref):
    rows = hf_ref.shape[0] // NP
    kp = hr_ref.shape[1]
    kf = fcs_ref.shape[0] // 2
    xs = _phases(hf_ref, rows) + _phases(hb_ref, rows)
    xcat = jnp.concatenate(xs, axis=1)
    hi, lo = _split_bf16(xcat)
    fcs = fcs_ref[...]
    a = (jnp.dot(fcs, hi, preferred_element_type=F32)
         + jnp.dot(fcs, lo, preferred_element_type=F32))
    c = cos_ref[...]
    s = sin_ref[...]

    def part(dirn, ph):
        j = (dirn * NP + ph) * LANES
        return a[:kp, j:j + LANES], a[kf:kf + kp, j:j + LANES]

    fr0, fi0 = part(0, 0)
    br0, bi0 = part(1, 0)
    gs = [(fr0 + br0, fi0 - bi0)]
    for d in range(1, NP):
        fr, fi = part(0, d)
        br, bi = part(1, NP - d)
        gr = fr + c * br + s * bi
        gi = fi + s * br - c * bi
        gs.append(_cmul_tw(gr, gi, twc_ref[d], tws_ref[d], False))
    hs = _fft_list(gs, -1)
    for k2 in range(NP):
        hr_ref[k2] = hs[k2][0] * (1.0 / NP)
        hi_ref[k2] = hs[k2][1] * (1.0 / NP)


def _filt_dft(hf, hb, fcs, cos_k, sin_k, twc, tws):
    seq, ncol = hf.shape
    kp = cos_k.shape[0]
    col = lambda j: (0, j)
    out_spec = pl.BlockSpec((NP, kp, LANES), lambda j: (0, 0, j))
    return pl.pallas_call(
        _filt_dft_kernel,
        grid=(ncol // LANES,),
        in_specs=[pl.BlockSpec((seq, LANES), col), pl.BlockSpec((seq, LANES), col),
                  _const_spec(fcs.shape), _const_spec(cos_k.shape),
                  _const_spec(sin_k.shape), _const_spec(twc.shape),
                  _const_spec(tws.shape)],
        out_specs=[out_spec, out_spec],
        out_shape=[jax.ShapeDtypeStruct((NP, kp, ncol), F32),
                   jax.ShapeDtypeStruct((NP, kp, ncol), F32)],
        compiler_params=_cparams(("parallel",)),
        name="filt_dft",
    )(hf, hb, fcs, cos_k, sin_k, twc, tws)


def _short_conv_phases(raw, w_ref, b_ref):
    rows = raw[0].shape[0]
    row = lax.broadcasted_iota(jnp.int32, raw[0].shape, 0)
    before = jnp.where(row == 0, 0.0, pltpu.roll(raw[NP - 1], 1, 0))
    after = jnp.where(row == rows - 1, 0.0, pltpu.roll(raw[0], rows - 1, 0))
    w0 = w_ref[0:1, :]
    w1 = w_ref[1:2, :]
    w2 = w_ref[2:3, :]
    b = b_ref[...]
    out = []
    for q in range(NP):
        prev = raw[q - 1] if q > 0 else before
        nxt = raw[q + 1] if q < NP - 1 else after
        out.append(prev * w0 + raw[q] * w1 + nxt * w2 + b)
    return out


def _longconv_kernel(x_ref, g_ref, wx_ref, bx_ref, wg_ref, bg_ref, bias_ref,
                     hr_ref, hi_ref, twc_ref, tws_ref, fcs_ref, ginv_ref, o_ref,
                     a_ref, b_ref, *, conv_x):
    rows = x_ref.shape[0] // NP
    kp = hr_ref.shape[1]
    kf = fcs_ref.shape[0] // 2
    xs = _phases(x_ref, rows)
    if conv_x:
        xs = _short_conv_phases(xs, wx_ref, bx_ref)
    gs = _short_conv_phases(_phases(g_ref, rows), wg_ref, bg_ref)

    xcat = jnp.concatenate([x.astype(BF16) for x in xs], axis=1)
    a_ref[...] = jnp.dot(fcs_ref[...], xcat, preferred_element_type=F32)

    def body(i):
        r0 = i * SUBLANES
        rs = pl.ds(r0, SUBLANES)
        zs = []
        for q in range(NP):
            ar = a_ref[rs, q * LANES:(q + 1) * LANES]
            ai = a_ref[pl.ds(kf + r0, SUBLANES), q * LANES:(q + 1) * LANES]
            zs.append((ar, ai) if q == 0 else
                      _cmul_tw(ar, ai, twc_ref[q, rs, :], tws_ref[q, rs, :], False))
        zs = _fft_list(zs, -1)
        ys = []
        for k2 in range(NP):
            zr, zi = zs[k2]
            hr = hr_ref[k2, rs, :]
            hi = hi_ref[k2, rs, :]
            ys.append((hr * zr - hi * zi, hr * zi + hi * zr))
        ys = _fft_list(ys, 1)
        for p in range(NP):
            br, bi = ys[p]
            if p > 0:
                br, bi = _cmul_tw(br, bi, twc_ref[p, rs, :], tws_ref[p, rs, :], True)
            b_ref[rs, p * LANES:(p + 1) * LANES] = br
            b_ref[pl.ds(kf + r0, SUBLANES), p * LANES:(p + 1) * LANES] = bi

    for i in range(kp // SUBLANES):
        body(i)
    if kf > kp:
        zeros = jnp.zeros((kf - kp, NP * LANES), F32)
        b_ref[kp:kf, :] = zeros
        b_ref[kf + kp:, :] = zeros

    y = jnp.dot(ginv_ref[...], b_ref[...].astype(BF16), preferred_element_type=F32)
    bias = bias_ref[...]
    for p in range(NP):
        yp = y[:, p * LANES:(p + 1) * LANES]
        o_ref[pl.ds(p, rows, stride=NP), :] = gs[p] * (yp + xs[p] * bias)


def _longconv(x3, x_blk0, hy3, g_blk0, sw, sb, wx_blk0, bias, h_blk0, hr, hi,
              twc, tws, fcs, ginv, conv_x):
    batch, seq, _ = hy3.shape
    nch = bias.shape[1] // LANES
    kp = hr.shape[1]
    kernel = functools.partial(_longconv_kernel, conv_x=conv_x)
    return pl.pallas_call(
        kernel,
        grid=(nch, batch),
        in_specs=[
            pl.BlockSpec((None, seq, LANES), lambda c, b: (b, 0, x_blk0 + c)),
            pl.BlockSpec((None, seq, LANES), lambda c, b: (b, 0, g_blk0 + c)),
            pl.BlockSpec((3, LANES), lambda c, b: (0, wx_blk0 + c)),
            pl.BlockSpec((1, LANES), lambda c, b: (0, wx_blk0 + c)),
            pl.BlockSpec((3, LANES), lambda c, b: (0, g_blk0 + c)),
            pl.BlockSpec((1, LANES), lambda c, b: (0, g_blk0 + c)),
            pl.BlockSpec((1, LANES), lambda c, b: (0, c)),
            pl.BlockSpec((NP, kp, LANES), lambda c, b: (0, 0, h_blk0 + c),
                         pipeline_mode=pl.Buffered(1)),
            pl.BlockSpec((NP, kp, LANES), lambda c, b: (0, 0, h_blk0 + c),
                         pipeline_mode=pl.Buffered(1)),
            _const_spec(twc.shape), _const_spec(tws.shape),
            _const_spec(fcs.shape), _const_spec(ginv.shape),
        ],
        out_specs=pl.BlockSpec((None, seq, LANES), lambda c, b: (b, 0, c)),
        out_shape=jax.ShapeDtypeStruct((batch, seq, nch * LANES), F32),
        scratch_shapes=[pltpu.VMEM((fcs.shape[0], NP * LANES), F32),
                        pltpu.VMEM((fcs.shape[0], NP * LANES), F32)],
        compiler_params=_cparams(("arbitrary", "arbitrary")),
        name="longconv_x" if conv_x else "longconv",
    )(x3, hy3, sw, sb, sw, sb, bias, hr, hi, twc, tws, fcs, ginv)


def _merge_kernel(x_ref, ya_ref, yb_ref, gpre_ref, gpost_ref, wgate_ref,
                  wa_ref, wb_ref, wo_ref, o_ref):
    x = x_ref[...]
    d = x.shape[-1]
    u = _rms(x, gpre_ref[...]).astype(BF16)
    gates = jnp.dot(u, wgate_ref[...], preferred_element_type=F32)
    ma = jnp.dot(ya_ref[...].astype(BF16), wa_ref[...], preferred_element_type=F32)
    mb = jnp.dot(yb_ref[...], wb_ref[...], preferred_element_type=F32)
    m = _sigmoid(gates[:, :d]) * ma + _sigmoid(gates[:, d:]) * mb
    y = jnp.dot(m.astype(BF16), wo_ref[...], preferred_element_type=F32)
    o_ref[...] = x + _rms(y, gpost_ref[...])


def _merge(x2d, ya, yb, gpre, gpost, wgate, wa, wb, wo, tm):
    n, d = x2d.shape
    row = lambda i: (i, 0)
    return pl.pallas_call(
        _merge_kernel,
        grid=(n // tm,),
        in_specs=[
            pl.BlockSpec((tm, d), row),
            pl.BlockSpec((tm, ya.shape[1]), row),
            pl.BlockSpec((tm, yb.shape[1]), row),
            _const_spec((1, d)), _const_spec((1, d)),
            _const_spec(wgate.shape), _const_spec(wa.shape),
            _const_spec(wb.shape), _const_spec(wo.shape),
        ],
        out_specs=pl.BlockSpec((tm, d), row),
        out_shape=jax.ShapeDtypeStruct((n, d), F32),
        compiler_params=_cparams(("parallel",)),
        name="merge",
    )(x2d, ya, yb, gpre, gpost, wgate, wa, wb, wo)


def _ffn_ple_kernel(x_ref, gpre_ref, gpost_ref, wg_ref, wu_ref, wd_ref,
                    p_ref, ppre_ref, ppost_ref, wpg_ref, wpp_ref, o_ref):
    x1 = _ffn_block(x_ref[...], gpre_ref, gpost_ref, wg_ref, wu_ref, wd_ref)
    u = _rms(x1, ppre_ref[...]).astype(BF16)
    g = _sigmoid(jnp.dot(u, wpg_ref[...], preferred_element_type=F32))
    e = jnp.dot(p_ref[...].astype(BF16), wpp_ref[...], preferred_element_type=F32)
    o_ref[...] = x1 + _rms(g * e, ppost_ref[...])


def _ffn_ple(x2d, gpre, gpost, wg, wu, wd, p2d, ppre, ppost, wpg, wpp, tm):
    n, d = x2d.shape
    row = lambda i: (i, 0)
    return pl.pallas_call(
        _ffn_ple_kernel,
        grid=(n // tm,),
        in_specs=[
            pl.BlockSpec((tm, d), row),
            _const_spec((1, d)), _const_spec((1, d)),
            _const_spec(wg.shape), _const_spec(wu.shape), _const_spec(wd.shape),
            pl.BlockSpec((tm, p2d.shape[1]), row),
            _const_spec((1, d)), _const_spec((1, d)),
            _const_spec(wpg.shape), _const_spec(wpp.shape),
        ],
        out_specs=pl.BlockSpec((tm, d), row),
        out_shape=jax.ShapeDtypeStruct((n, d), F32),
        compiler_params=_cparams(("parallel",)),
        name="ffn_ple",
    )(x2d, gpre, gpost, wg, wu, wd, p2d, ppre, ppost, wpg, wpp)


def _rope_tables(seq):
    rows = seq // GRID_W
    row = np.repeat(np.arange(rows, dtype=np.float32), GRID_W)
    col = np.tile(np.arange(GRID_W, dtype=np.float32), rows)
    inv = (ROPE_THETA ** (-np.arange(0, AXIS_DIM, 2, dtype=np.float32) / AXIS_DIM)
           ).astype(np.float32)
    ang = np.concatenate([row[:, None] * inv, col[:, None] * inv], axis=-1)
    cos = np.repeat(np.cos(ang), 2, axis=1)
    sin = np.repeat(np.sin(ang), 2, axis=1)
    sign = np.tile(np.array([-1.0, 1.0], np.float32), HEAD_DIM // 2)
    cos_t = np.tile(cos, (1, N_KV_HEADS)).astype(np.float32)
    sin_t = np.tile(sin * sign, (1, N_KV_HEADS)).astype(np.float32)
    return jnp.asarray(cos_t), jnp.asarray(sin_t)


def _block_diag_mean(width):
    idx = np.arange(width) // HEAD_DIM
    return jnp.asarray((idx[:, None] == idx[None, :]).astype(np.float32) / HEAD_DIM,
                       dtype=BF16)


def _filter_features(seq, pad_to):
    t = np.linspace(0.0, 1.0, seq, dtype=np.float32)[:, None]
    w = ((2.0 * math.pi / seq) * np.arange(seq, dtype=np.float32)).astype(np.float32)
    bands = np.linspace(1e-4, FILT_BANDS - 1, FILT_BANDS, dtype=np.float32)
    ang = w[:, None] * bands[None, :]
    z = np.concatenate([t, np.cos(ang), -np.sin(ang)], axis=-1).astype(np.float32)
    return jnp.asarray(np.pad(z, ((0, 0), (0, pad_to - z.shape[1]))))


def _dft_tables(seq):
    m1 = seq // NP
    nf = 2 * m1
    nk = nf // 2 + 1
    kp = -(-nk // SUBLANES) * SUBLANES
    half_tile = LANES // 2
    kf = -(-nk // half_tile) * half_tile
    k = np.arange(nk, dtype=np.float64)[:, None]
    m = np.arange(m1, dtype=np.float64)[None, :]
    th = 2.0 * np.pi * k * m / nf
    fcs = np.zeros((2 * kf, m1), np.float64)
    fcs[:nk] = np.cos(th)
    fcs[kf:kf + nk] = -np.sin(th)
    ck = np.full((nk,), 2.0 / nf)
    ck[0] = 1.0 / nf
    ck[-1] = 1.0 / nf
    ginv = np.zeros((m1, 2 * kf), np.float64)
    ginv[:, :nk] = (np.cos(th) * ck[:, None]).T
    ginv[:, kf:kf + nk] = (-np.sin(th) * ck[:, None]).T
    th1 = 2.0 * np.pi * np.arange(kp, dtype=np.float64) / nf
    th1[nk:] = 0.0
    cos_k = np.repeat(np.cos(th1)[:, None], LANES, axis=1)
    sin_k = np.repeat(np.sin(th1)[:, None], LANES, axis=1)
    thq = th1[None, :] * np.arange(NP, dtype=np.float64)[:, None] / NP
    twc = np.repeat(np.cos(thq)[:, :, None], LANES, axis=2)
    tws = np.repeat(np.sin(thq)[:, :, None], LANES, axis=2)
    return tuple(jnp.asarray(t, dtype=F32) for t in (fcs, ginv, cos_k, sin_k, twc, tws))


def _pick_tile(n, want):
    t = min(n, want)
    while n % t:
        t //= 2
    return t


def kernel(x, p, ffn1_norm_pre, ffn1_norm_post, ffn1_w_gate, ffn1_w_up, ffn1_w_down,
           mix_norm_pre, mix_norm_post, w_in, hy_short_w, hy_short_b,
           filt_w1, filt_b1, filt_freq1, filt_w2, filt_b2, filt_freq2, filt_w3,
           filt_deltas, hy_bias, q_norm, k_norm, w_hy_out, w_att_out, w_out,
           ffn2_norm_pre, ffn2_norm_post, ffn2_w_gate, ffn2_w_up, ffn2_w_down,
           ple_norm_pre, ple_norm_post, w_ple_gate, w_ple_proj):
    batch, seq, d = x.shape
    depth = p.shape[0]
    n = batch * seq
    nhy = 3 * HY_WIDTH
    nmix = nhy + ATT_Q + 2 * ATT_KV
    hy_blocks = HY_WIDTH // LANES

    tm = _pick_tile(seq, 512)
    tq = _pick_tile(seq, 512)
    cos_t, sin_t = _rope_tables(seq)
    bdq = _block_diag_mean(ATT_Q)
    bdk = _block_diag_mean(ATT_KV)
    ze = LANES
    zf = _filter_features(seq, ze)
    row = lambda a: a.reshape(1, -1).astype(F32)
    bf = lambda a: a.astype(BF16)
    fcs, ginv, cos_k, sin_k, twc, tws = _dft_tables(seq)
    fcs, ginv = bf(fcs), bf(ginv)

    x2d = x.reshape(n, d)
    for i in range(depth):
        x2d = _ffn(x2d, row(ffn1_norm_pre[i]), row(ffn1_norm_post[i]),
                   bf(ffn1_w_gate[i]), bf(ffn1_w_up[i]), bf(ffn1_w_down[i]), tm)

        hy, q, k, v = _inproj(
            x2d, row(mix_norm_pre[i]), bf(w_in[i][:, :nmix]),
            row(jnp.tile(q_norm[i], N_Q_HEADS)), row(jnp.tile(k_norm[i], N_KV_HEADS)),
            bdq, bdk, cos_t, sin_t, seq, tm)

        w3 = filt_w3[i].reshape(FILT_HID, HY_ORDER, 2, HY_WIDTH)
        dl = filt_deltas[i]
        w1p = jnp.pad(filt_w1[i], ((0, ze - FILT_EMB), (0, 0)))
        hf, hb = _filt_mlp(
            zf, w1p, row(filt_b1[i]), row(filt_freq1[i]), filt_w2[i],
            row(filt_b2[i]), row(filt_freq2[i]),
            w3[:, :, 0].reshape(FILT_HID, -1), w3[:, :, 1].reshape(FILT_HID, -1),
            row(dl[:, 0]), row(dl[:, 1]), LANES)
        hr, hi = _filt_dft(hf, hb, fcs, cos_k, sin_k, twc, tws)

        hy3 = hy.reshape(batch, seq, nhy)
        sw = hy_short_w[i]
        sb = row(hy_short_b[i])
        z1 = _longconv(hy3, 0, hy3, hy_blocks, sw, sb, 0, row(hy_bias[i][0]),
                       0, hr, hi, twc, tws, fcs, ginv, True)
        ya = _longconv(z1, 0, hy3, 2 * hy_blocks, sw, sb, 0, row(hy_bias[i][1]),
                       hy_blocks, hr, hi, twc, tws, fcs, ginv, False)

        yb = _attention(q, k, v, bdq, bdk, batch, seq, tq)

        x2d = _merge(x2d, ya.reshape(n, HY_WIDTH), yb, row(mix_norm_pre[i]),
                     row(mix_norm_post[i]), bf(w_in[i][:, nmix:]), bf(w_hy_out[i]),
                     bf(w_att_out[i]), bf(w_out[i]), tm)

        x2d = _ffn_ple(x2d, row(ffn2_norm_pre[i]), row(ffn2_norm_post[i]),
                       bf(ffn2_w_gate[i]), bf(ffn2_w_up[i]), bf(ffn2_w_down[i]),
                       p[i].reshape(n, -1), row(ple_norm_pre[i]), row(ple_norm_post[i]),
                       bf(w_ple_gate[i]), bf(w_ple_proj[i]), tm)
    return x2d.reshape(batch, seq, d)
```

```python
import functools
import math

import numpy as np
import jax
import jax.numpy as jnp
from jax import lax
from jax.experimental import pallas as pl
from jax.experimental.pallas import tpu as pltpu

F32 = jnp.float32
BF16 = jnp.bfloat16

GRID_W = 64
HY_WIDTH = 512
HY_ORDER = 2
FILT_EMB = 33
FILT_BANDS = (FILT_EMB - 1) // 2
FILT_HID = 64
N_Q_HEADS = 8
N_KV_HEADS = 2
GROUP = N_Q_HEADS // N_KV_HEADS
HEAD_DIM = 64
AXIS_DIM = HEAD_DIM // 2
ROPE_THETA = 10000.0
EPS = 1e-6
ATT_Q = N_Q_HEADS * HEAD_DIM
ATT_KV = N_KV_HEADS * HEAD_DIM

LANES = 128
SUBLANES = 8
VMEM_LIMIT_BYTES = 60 * 1024 * 1024

NP = 8


def _cparams(sem):
    return pltpu.CompilerParams(dimension_semantics=sem,
                                vmem_limit_bytes=VMEM_LIMIT_BYTES)


def _const_spec(shape):
    nd = len(shape)
    return pl.BlockSpec(shape, lambda *_: (0,) * nd, pipeline_mode=pl.Buffered(1))


def _rms(x, g):
    return x * lax.rsqrt(jnp.mean(x * x, axis=-1, keepdims=True) + EPS) * g


def _sigmoid(x):
    return 1.0 / (1.0 + jnp.exp(-x))


def _ffn_block(x, gpre_ref, gpost_ref, wg_ref, wu_ref, wd_ref):
    xn = _rms(x, gpre_ref[...]).astype(BF16)
    g = jnp.dot(xn, wg_ref[...], preferred_element_type=F32)
    u = jnp.dot(xn, wu_ref[...], preferred_element_type=F32)
    h = (g * _sigmoid(g) * u).astype(BF16)
    y = jnp.dot(h, wd_ref[...], preferred_element_type=F32)
    return x + 0.5 * _rms(y, gpost_ref[...])


def _ffn_kernel(x_ref, gpre_ref, gpost_ref, wg_ref, wu_ref, wd_ref, o_ref):
    o_ref[...] = _ffn_block(x_ref[...], gpre_ref, gpost_ref, wg_ref, wu_ref, wd_ref)


def _ffn(x2d, gpre, gpost, wg, wu, wd, tm):
    n, d = x2d.shape
    ff = wg.shape[1]
    return pl.pallas_call(
        _ffn_kernel,
        grid=(n // tm,),
        in_specs=[
            pl.BlockSpec((tm, d), lambda i: (i, 0)),
            _const_spec((1, d)), _const_spec((1, d)),
            _const_spec((d, ff)), _const_spec((d, ff)), _const_spec((ff, d)),
        ],
        out_specs=pl.BlockSpec((tm, d), lambda i: (i, 0)),
        out_shape=jax.ShapeDtypeStruct((n, d), F32),
        compiler_params=_cparams(("parallel",)),
        name="ffn",
    )(x2d, gpre, gpost, wg, wu, wd)


Q_SCALE = HEAD_DIM ** -0.5 * math.log2(math.e)


def _head_norm_rope(t, gain, bd, cos, sin):
    ms = jnp.dot((t * t).astype(BF16), bd, preferred_element_type=F32)
    tn = t * lax.rsqrt(ms + EPS) * gain
    w = t.shape[-1]
    lane = lax.broadcasted_iota(jnp.int32, tn.shape, 1)
    nxt = pltpu.roll(tn, w - 1, 1)
    prv = pltpu.roll(tn, 1, 1)
    swapped = jnp.where((lane & 1) == 0, nxt, prv)
    return tn * cos + swapped * sin


def _inproj_kernel(x_ref, gpre_ref, w_ref, qg_ref, kg_ref, bdq_ref, bdk_ref,
                   cos_ref, sin_ref, hy_ref, q_ref, k_ref, v_ref):
    nhy = hy_ref.shape[-1]
    u = _rms(x_ref[...], gpre_ref[...]).astype(BF16)
    proj = jnp.dot(u, w_ref[...], preferred_element_type=F32)
    hy_ref[...] = proj[:, :nhy]
    q = proj[:, nhy:nhy + ATT_Q]
    k = proj[:, nhy + ATT_Q:nhy + ATT_Q + ATT_KV]
    v = proj[:, nhy + ATT_Q + ATT_KV:]
    cos_k = cos_ref[...]
    sin_k = sin_ref[...]
    reps = ATT_Q // ATT_KV
    cos_q = jnp.concatenate([cos_k] * reps, axis=1)
    sin_q = jnp.concatenate([sin_k] * reps, axis=1)
    qr = _head_norm_rope(q, qg_ref[...], bdq_ref[...], cos_q, sin_q)
    kr = _head_norm_rope(k, kg_ref[...], bdk_ref[...], cos_k, sin_k)
    q_ref[...] = (qr * Q_SCALE).astype(BF16)
    k_ref[...] = kr.astype(BF16)
    v_ref[...] = v.astype(BF16)


def _inproj(x2d, gpre, w, qg, kg, bdq, bdk, cos_t, sin_t, seq, tm):
    n, d = x2d.shape
    ncols = w.shape[1]
    nhy = ncols - ATT_Q - 2 * ATT_KV
    tiles_per_seq = seq // tm
    return pl.pallas_call(
        _inproj_kernel,
        grid=(n // tm,),
        in_specs=[
            pl.BlockSpec((tm, d), lambda i: (i, 0)),
            _const_spec((1, d)),
            _const_spec((d, ncols)),
            _const_spec((1, ATT_Q)), _const_spec((1, ATT_KV)),
            _const_spec((ATT_Q, ATT_Q)), _const_spec((ATT_KV, ATT_KV)),
            pl.BlockSpec((tm, ATT_KV), lambda i: (i % tiles_per_seq, 0)),
            pl.BlockSpec((tm, ATT_KV), lambda i: (i % tiles_per_seq, 0)),
        ],
        out_specs=[
            pl.BlockSpec((tm, nhy), lambda i: (i, 0)),
            pl.BlockSpec((tm, ATT_Q), lambda i: (i, 0)),
            pl.BlockSpec((tm, ATT_KV), lambda i: (i, 0)),
            pl.BlockSpec((tm, ATT_KV), lambda i: (i, 0)),
        ],
        out_shape=[
            jax.ShapeDtypeStruct((n, nhy), F32),
            jax.ShapeDtypeStruct((n, ATT_Q), BF16),
            jax.ShapeDtypeStruct((n, ATT_KV), BF16),
            jax.ShapeDtypeStruct((n, ATT_KV), BF16),
        ],
        compiler_params=_cparams(("parallel",)),
        name="inproj",
    )(x2d, gpre, w, qg, kg, bdq, bdk, cos_t, sin_t)


SHIFT_BOUND_MAX = 60.0
BOUND_SLACK = 1.02


def _attn_kernel(shift_ref, q_ref, k_ref, v_ref, o_ref, k0_ref, v0_ref, *, use_shift):
    gw = GROUP * HEAD_DIM
    tq = q_ref.shape[0]

    @pl.when(pl.program_id(1) == 0)
    def _():
        lane = lax.broadcasted_iota(jnp.int32, (k_ref.shape[0], gw), 1)
        reps = gw // ATT_KV
        kk = jnp.concatenate([k_ref[...].astype(F32)] * reps, axis=1)
        vv = jnp.concatenate([v_ref[...].astype(F32)] * reps, axis=1)
        for g in range(N_KV_HEADS):
            ks = kk if g == 0 else pltpu.roll(kk, gw - g * HEAD_DIM, 1)
            vs = vv if g == 0 else pltpu.roll(vv, gw - g * HEAD_DIM, 1)
            k0_ref[g] = jnp.where(lane < HEAD_DIM, ks, 0.0).astype(BF16)
            v0_ref[g] = jnp.where(lane < HEAD_DIM, vs,
                                  jnp.where(lane == HEAD_DIM, 1.0, 0.0)).astype(BF16)

    lane_o = lax.broadcasted_iota(jnp.int32, (tq, gw), 1)
    shift = shift_ref[0:1, 0:1]
    for g in range(N_KV_HEADS):
        qg = q_ref[:, g * gw:(g + 1) * gw].astype(F32)
        acc = jnp.zeros((tq, gw), F32)
        for h in range(GROUP):
            qh = qg if h == 0 else pltpu.roll(qg, gw - h * HEAD_DIM, 1)
            s = lax.dot_general(qh.astype(BF16), k0_ref[g], (((1,), (1,)), ((), ())),
                                preferred_element_type=F32)
            m = shift if use_shift else jnp.max(s, axis=-1, keepdims=True)
            p = jnp.exp2(s - m)
            o = jnp.dot(p.astype(BF16), v0_ref[g], preferred_element_type=F32)
            l = o[:, HEAD_DIM:HEAD_DIM + 1]
            o = jnp.where(lane_o < HEAD_DIM, o * (1.0 / l), 0.0)
            acc = acc + (o if h == 0 else pltpu.roll(o, h * HEAD_DIM, 1))
        o_ref[:, g * gw:(g + 1) * gw] = acc.astype(BF16)


def _attention_call(shift, q, k, v, *, batch, seq, tq, use_shift):
    n = q.shape[0]
    gw = GROUP * HEAD_DIM
    qt = seq // tq
    return pl.pallas_call(
        functools.partial(_attn_kernel, use_shift=use_shift),
        grid=(batch, qt),
        in_specs=[
            _const_spec(shift.shape),
            pl.BlockSpec((tq, ATT_Q), lambda b, i: (b * qt + i, 0)),
            pl.BlockSpec((seq, ATT_KV), lambda b, i: (b, 0)),
            pl.BlockSpec((seq, ATT_KV), lambda b, i: (b, 0)),
        ],
        out_specs=pl.BlockSpec((tq, ATT_Q), lambda b, i: (b * qt + i, 0)),
        out_shape=jax.ShapeDtypeStruct((n, ATT_Q), BF16),
        scratch_shapes=[pltpu.VMEM((N_KV_HEADS, seq, gw), BF16),
                        pltpu.VMEM((N_KV_HEADS, seq, gw), BF16)],
        compiler_params=_cparams(("arbitrary", "arbitrary")),
        name="attention_shift" if use_shift else "attention_max",
    )(shift, q, k, v)


def _attention(q, k, v, q_gain, k_gain, batch, seq, tq):
    bound = ((HEAD_DIM * Q_SCALE * BOUND_SLACK) * jnp.max(jnp.abs(q_gain))
             * jnp.max(jnp.abs(k_gain))).astype(F32)
    shift = jnp.full((SUBLANES, LANES), bound, F32)
    call = functools.partial(_attention_call, batch=batch, seq=seq, tq=tq)
    return lax.cond(bound < SHIFT_BOUND_MAX,
                    functools.partial(call, use_shift=True),
                    functools.partial(call, use_shift=False),
                    shift, q, k, v)


def _filt_mlp_kernel(z_ref, w1_ref, b1_ref, f1_ref, w2_ref, b2_ref, f2_ref,
                     w3f_ref, w3b_ref, df_ref, db_ref, hf_ref, hb_ref, h_ref):
    hp = lax.Precision.HIGHEST

    @pl.when(pl.program_id(0) == 0)
    def _():
        h1 = jnp.sin(f1_ref[...] * (jnp.dot(z_ref[...], w1_ref[...], precision=hp,
                                            preferred_element_type=F32) + b1_ref[...]))
        h_ref[...] = jnp.sin(f2_ref[...] * (jnp.dot(h1, w2_ref[...], precision=hp,
                                                    preferred_element_type=F32)
                                            + b2_ref[...]))

    h = h_ref[...]
    t = z_ref[:, 0:1]
    hf = jnp.dot(h, w3f_ref[...], precision=hp, preferred_element_type=F32)
    hb = jnp.dot(h, w3b_ref[...], precision=hp, preferred_element_type=F32)
    hf = hf * jnp.exp(-t * jnp.abs(df_ref[...]))
    hb = hb * jnp.exp(-t * jnp.abs(db_ref[...]))
    row = lax.broadcasted_iota(jnp.int32, hf.shape, 0)
    both = jnp.abs(hf) + jnp.abs(hb)
    merged = jnp.abs(hf + hb)
    s = jnp.sum(jnp.where(row == 0, merged, both), axis=0, keepdims=True)
    inv = 1.0 / s
    hf_ref[...] = hf * inv
    hb_ref[...] = hb * inv


def _filt_mlp(zf, w1, b1, f1, w2, b2, f2, w3f, w3b, df, db, tc):
    seq, ze = zf.shape
    ncol = w3f.shape[1]
    hid = w2.shape[0]
    col = lambda j: (0, j)
    return pl.pallas_call(
        _filt_mlp_kernel,
        grid=(ncol // tc,),
        in_specs=[
            _const_spec((seq, ze)), _const_spec((ze, hid)), _const_spec((1, hid)),
            _const_spec((1, hid)), _const_spec((hid, hid)), _const_spec((1, hid)),
            _const_spec((1, hid)),
            pl.BlockSpec((hid, tc), col), pl.BlockSpec((hid, tc), col),
            pl.BlockSpec((1, tc), col), pl.BlockSpec((1, tc), col),
        ],
        out_specs=[pl.BlockSpec((seq, tc), col), pl.BlockSpec((seq, tc), col)],
        out_shape=[jax.ShapeDtypeStruct((seq, ncol), F32),
                   jax.ShapeDtypeStruct((seq, ncol), F32)],
        scratch_shapes=[pltpu.VMEM((seq, hid), F32)],
        compiler_params=_cparams(("arbitrary",)),
        name="filt_mlp",
    )(zf, w1, b1, f1, w2, b2, f2, w3f, w3b, df, db)


def _phases(ref, rows):
    return [ref[pl.ds(q, rows, stride=NP), :] for q in range(NP)]


def _split_bf16(x):
    hi = x.astype(BF16)
    lo = (x - hi.astype(F32)).astype(BF16)
    return hi, lo


def _cmul_tw(xr, xi, c, s, conj):
    if conj:
        return xr * c - xi * s, xi * c + xr * s
    return xr * c + xi * s, xi * c - xr * s


def _fft_list(xs, sign):
    n = len(xs)
    if n == 1:
        return xs
    even = _fft_list(xs[0::2], sign)
    odd = _fft_list(xs[1::2], sign)
    half = n // 2
    out = [None] * n
    for k in range(half):
        er, ei = even[k]
        o_r, o_i = odd[k]
        if k == 0:
            tr, ti = o_r, o_i
            out[k] = (er + tr, ei + ti)
            out[k + half] = (er - tr, ei - ti)
        elif 4 * k == n:
            if sign > 0:
                out[k] = (er - o_i, ei + o_r)
                out[k + half] = (er + o_i, ei - o_r)
            else:
                out[k] = (er + o_i, ei - o_r)
                out[k + half] = (er - o_i, ei + o_r)
        else:
            c = math.cos(2.0 * math.pi * k / n)
            s = sign * math.sin(2.0 * math.pi * k / n)
            tr = o_r * c - o_i * s
            ti = o_r * s + o_i * c
            out[k] = (er + tr, ei + ti)
            out[k + half] = (er - tr, ei - ti)
    return out


def _filt_dft_kernel(hf_ref, hb_ref, fcs_ref, cos_ref, sin_ref, twc_ref, tws_ref,
                     hr_ref, hi_ref):
    rows = hf_ref.shape[0] // NP
    kp = hr_ref.shape[1]
    kf = fcs_ref.shape[0] // 2
    xs = _phases(hf_ref, rows) + _phases(hb_ref, rows)
    xcat = jnp.concatenate(xs, axis=1)
    hi, lo = _split_bf16(xcat)
    fcs = fcs_ref[...]
    a = (jnp.dot(fcs, hi, preferred_element_type=F32)
         + jnp.dot(fcs, lo, preferred_element_type=F32))
    c = cos_ref[...]
    s = sin_ref[...]

    def part(dirn, ph):
        j = (dirn * NP + ph) * LANES
        return a[:kp, j:j + LANES], a[kf:kf + kp, j:j + LANES]

    fr0, fi0 = part(0, 0)
    br0, bi0 = part(1, 0)
    gs = [(fr0 + br0, fi0 - bi0)]
    for d in range(1, NP):
        fr, fi = part(0, d)
        br, bi = part(1, NP - d)
        gr = fr + c * br + s * bi
        gi = fi + s * br - c * bi
        gs.append(_cmul_tw(gr, gi, twc_ref[d], tws_ref[d], False))
    hs = _fft_list(gs, -1)
    for k2 in range(NP):
        hr_ref[k2] = hs[k2][0] * (1.0 / NP)
        hi_ref[k2] = hs[k2][1] * (1.0 / NP)


def _filt_dft(hf, hb, fcs, cos_k, sin_k, twc, tws):
    seq, ncol = hf.shape
    kp = cos_k.shape[0]
    col = lambda j: (0, j)
    out_spec = pl.BlockSpec((NP, kp, LANES), lambda j: (0, 0, j))
    return pl.pallas_call(
        _filt_dft_kernel,
        grid=(ncol // LANES,),
        in_specs=[pl.BlockSpec((seq, LANES), col), pl.BlockSpec((seq, LANES), col),
                  _const_spec(fcs.shape), _const_spec(cos_k.shape),
                  _const_spec(sin_k.shape), _const_spec(twc.shape),
                  _const_spec(tws.shape)],
        out_specs=[out_spec, out_spec],
        out_shape=[jax.ShapeDtypeStruct((NP, kp, ncol), F32),
                   jax.ShapeDtypeStruct((NP, kp, ncol), F32)],
        compiler_params=_cparams(("parallel",)),
        name="filt_dft",
    )(hf, hb, fcs, cos_k, sin_k, twc, tws)


def _short_conv_phases(raw, w_ref, b_ref):
    rows = raw[0].shape[0]
    row = lax.broadcasted_iota(jnp.int32, raw[0].shape, 0)
    before = jnp.where(row == 0, 0.0, pltpu.roll(raw[NP - 1], 1, 0))
    after = jnp.where(row == rows - 1, 0.0, pltpu.roll(raw[0], rows - 1, 0))
    w0 = w_ref[0:1, :]
    w1 = w_ref[1:2, :]
    w2 = w_ref[2:3, :]
    b = b_ref[...]
    out = []
    for q in range(NP):
        prev = raw[q - 1] if q > 0 else before
        nxt = raw[q + 1] if q < NP - 1 else after
        out.append(prev * w0 + raw[q] * w1 + nxt * w2 + b)
    return out


def _longconv_kernel(x_ref, g_ref, wx_ref, bx_ref, wg_ref, bg_ref, bias_ref,
                     hr_ref, hi_ref, twc_ref, tws_ref, fcs_ref, ginv_ref, o_ref,
                     a_ref, b_ref, *, conv_x):
    rows = x_ref.shape[0] // NP
    kp = hr_ref.shape[1]
    kf = fcs_ref.shape[0] // 2
    xs = _phases(x_ref, rows)
    if conv_x:
        xs = _short_conv_phases(xs, wx_ref, bx_ref)
    gs = _short_conv_phases(_phases(g_ref, rows), wg_ref, bg_ref)

    xcat = jnp.concatenate([x.astype(BF16) for x in xs], axis=1)
    a_ref[...] = jnp.dot(fcs_ref[...], xcat, preferred_element_type=F32)

    def body(i):
        r0 = i * SUBLANES
        rs = pl.ds(r0, SUBLANES)
        zs = []
        for q in range(NP):
            ar = a_ref[rs, q * LANES:(q + 1) * LANES]
            ai = a_ref[pl.ds(kf + r0, SUBLANES), q * LANES:(q + 1) * LANES]
            zs.append((ar, ai) if q == 0 else
                      _cmul_tw(ar, ai, twc_ref[q, rs, :], tws_ref[q, rs, :], False))
        zs = _fft_list(zs, -1)
        ys = []
        for k2 in range(NP):
            zr, zi = zs[k2]
            hr = hr_ref[k2, rs, :]
            hi = hi_ref[k2, rs, :]
            ys.append((hr * zr - hi * zi, hr * zi + hi * zr))
        ys = _fft_list(ys, 1)
        for p in range(NP):
            br, bi = ys[p]
            if p > 0:
                br, bi = _cmul_tw(br, bi, twc_ref[p, rs, :], tws_ref[p, rs, :], True)
            b_ref[rs, p * LANES:(p + 1) * LANES] = br
            b_ref[pl.ds(kf + r0, SUBLANES), p * LANES:(p + 1) * LANES] = bi

    for i in range(kp // SUBLANES):
        body(i)
    if kf > kp:
        zeros = jnp.zeros((kf - kp, NP * LANES), F32)
        b_ref[kp:kf, :] = zeros
        b_ref[kf + kp:, :] = zeros

    y = jnp.dot(ginv_ref[...], b_ref[...].astype(BF16), preferred_element_type=F32)
    bias = bias_ref[...]
    for p in range(NP):
        yp = y[:, p * LANES:(p + 1) * LANES]
        o_ref[pl.ds(p, rows, stride=NP), :] = gs[p] * (yp + xs[p] * bias)


def _longconv(x3, x_blk0, hy3, g_blk0, sw, sb, wx_blk0, bias, h_blk0, hr, hi,
              twc, tws, fcs, ginv, conv_x):
    batch, seq, _ = hy3.shape
    nch = bias.shape[1] // LANES
    kp = hr.shape[1]
    kernel = functools.partial(_longconv_kernel, conv_x=conv_x)
    return pl.pallas_call(
        kernel,
        grid=(nch, batch),
        in_specs=[
            pl.BlockSpec((None, seq, LANES), lambda c, b: (b, 0, x_blk0 + c)),
            pl.BlockSpec((None, seq, LANES), lambda c, b: (b, 0, g_blk0 + c)),
            pl.BlockSpec((3, LANES), lambda c, b: (0, wx_blk0 + c)),
            pl.BlockSpec((1, LANES), lambda c, b: (0, wx_blk0 + c)),
            pl.BlockSpec((3, LANES), lambda c, b: (0, g_blk0 + c)),
            pl.BlockSpec((1, LANES), lambda c, b: (0, g_blk0 + c)),
            pl.BlockSpec((1, LANES), lambda c, b: (0, c)),
            pl.BlockSpec((NP, kp, LANES), lambda c, b: (0, 0, h_blk0 + c),
                         pipeline_mode=pl.Buffered(1)),
            pl.BlockSpec((NP, kp, LANES), lambda c, b: (0, 0, h_blk0 + c),
                         pipeline_mode=pl.Buffered(1)),
            _const_spec(twc.shape), _const_spec(tws.shape),
            _const_spec(fcs.shape), _const_spec(ginv.shape),
        ],
        out_specs=pl.BlockSpec((None, seq, LANES), lambda c, b: (b, 0, c)),
        out_shape=jax.ShapeDtypeStruct((batch, seq, nch * LANES), F32),
        scratch_shapes=[pltpu.VMEM((fcs.shape[0], NP * LANES), F32),
                        pltpu.VMEM((fcs.shape[0], NP * LANES), F32)],
        compiler_params=_cparams(("arbitrary", "arbitrary")),
        name="longconv_x" if conv_x else "longconv",
    )(x3, hy3, sw, sb, sw, sb, bias, hr, hi, twc, tws, fcs, ginv)


def _merge_kernel(x_ref, ya_ref, yb_ref, gpre_ref, gpost_ref, wgate_ref,
                  wa_ref, wb_ref, wo_ref, o_ref):
    x = x_ref[...]
    d = x.shape[-1]
    u = _rms(x, gpre_ref[...]).astype(BF16)
    gates = jnp.dot(u, wgate_ref[...], preferred_element_type=F32)
    ma = jnp.dot(ya_ref[...].astype(BF16), wa_ref[...], preferred_element_type=F32)
    mb = jnp.dot(yb_ref[...], wb_ref[...], preferred_element_type=F32)
    m = _sigmoid(gates[:, :d]) * ma + _sigmoid(gates[:, d:]) * mb
    y = jnp.dot(m.astype(BF16), wo_ref[...], preferred_element_type=F32)
    o_ref[...] = x + _rms(y, gpost_ref[...])


def _merge(x2d, ya, yb, gpre, gpost, wgate, wa, wb, wo, tm):
    n, d = x2d.shape
    row = lambda i: (i, 0)
    return pl.pallas_call(
        _merge_kernel,
        grid=(n // tm,),
        in_specs=[
            pl.BlockSpec((tm, d), row),
            pl.BlockSpec((tm, ya.shape[1]), row),
            pl.BlockSpec((tm, yb.shape[1]), row),
            _const_spec((1, d)), _const_spec((1, d)),
            _const_spec(wgate.shape), _const_spec(wa.shape),
            _const_spec(wb.shape), _const_spec(wo.shape),
        ],
        out_specs=pl.BlockSpec((tm, d), row),
        out_shape=jax.ShapeDtypeStruct((n, d), F32),
        compiler_params=_cparams(("parallel",)),
        name="merge",
    )(x2d, ya, yb, gpre, gpost, wgate, wa, wb, wo)


def _ffn_ple_kernel(x_ref, gpre_ref, gpost_ref, wg_ref, wu_ref, wd_ref,
                    p_ref, ppre_ref, ppost_ref, wpg_ref, wpp_ref, o_ref):
    x1 = _ffn_block(x_ref[...], gpre_ref, gpost_ref, wg_ref, wu_ref, wd_ref)
    u = _rms(x1, ppre_ref[...]).astype(BF16)
    g = _sigmoid(jnp.dot(u, wpg_ref[...], preferred_element_type=F32))
    e = jnp.dot(p_ref[...].astype(BF16), wpp_ref[...], preferred_element_type=F32)
    o_ref[...] = x1 + _rms(g * e, ppost_ref[...])


def _ffn_ple(x2d, gpre, gpost, wg, wu, wd, p2d, ppre, ppost, wpg, wpp, tm):
    n, d = x2d.shape
    row = lambda i: (i, 0)
    return pl.pallas_call(
        _ffn_ple_kernel,
        grid=(n // tm,),
        in_specs=[
            pl.BlockSpec((tm, d), row),
            _const_spec((1, d)), _const_spec((1, d)),
            _const_spec(wg.shape), _const_spec(wu.shape), _const_spec(wd.shape),
            pl.BlockSpec((tm, p2d.shape[1]), row),
            _const_spec((1, d)), _const_spec((1, d)),
            _const_spec(wpg.shape), _const_spec(wpp.shape),
        ],
        out_specs=pl.BlockSpec((tm, d), row),
        out_shape=jax.ShapeDtypeStruct((n, d), F32),
        compiler_params=_cparams(("parallel",)),
        name="ffn_ple",
    )(x2d, gpre, gpost, wg, wu, wd, p2d, ppre, ppost, wpg, wpp)


def _rope_tables(seq):
    rows = seq // GRID_W
    row = np.repeat(np.arange(rows, dtype=np.float32), GRID_W)
    col = np.tile(np.arange(GRID_W, dtype=np.float32), rows)
    inv = (ROPE_THETA ** (-np.arange(0, AXIS_DIM, 2, dtype=np.float32) / AXIS_DIM)
           ).astype(np.float32)
    ang = np.concatenate([row[:, None] * inv, col[:, None] * inv], axis=-1)
    cos = np.repeat(np.cos(ang), 2, axis=1)
    sin = np.repeat(np.sin(ang), 2, axis=1)
    sign = np.tile(np.array([-1.0, 1.0], np.float32), HEAD_DIM // 2)
    cos_t = np.tile(cos, (1, N_KV_HEADS)).astype(np.float32)
    sin_t = np.tile(sin * sign, (1, N_KV_HEADS)).astype(np.float32)
    return jnp.asarray(cos_t), jnp.asarray(sin_t)


def _block_diag_mean(width):
    idx = np.arange(width) // HEAD_DIM
    return jnp.asarray((idx[:, None] == idx[None, :]).astype(np.float32) / HEAD_DIM,
                       dtype=BF16)


def _filter_features(seq, pad_to):
    t = np.linspace(0.0, 1.0, seq, dtype=np.float32)[:, None]
    w = ((2.0 * math.pi / seq) * np.arange(seq, dtype=np.float32)).astype(np.float32)
    bands = np.linspace(1e-4, FILT_BANDS - 1, FILT_BANDS, dtype=np.float32)
    ang = w[:, None] * bands[None, :]
    z = np.concatenate([t, np.cos(ang), -np.sin(ang)], axis=-1).astype(np.float32)
    return jnp.asarray(np.pad(z, ((0, 0), (0, pad_to - z.shape[1]))))


def _dft_tables(seq):
    m1 = seq // NP
    nf = 2 * m1
    nk = nf // 2 + 1
    kp = -(-nk // SUBLANES) * SUBLANES
    half_tile = LANES // 2
    kf = -(-nk // half_tile) * half_tile
    k = np.arange(nk, dtype=np.float64)[:, None]
    m = np.arange(m1, dtype=np.float64)[None, :]
    th = 2.0 * np.pi * k * m / nf
    fcs = np.zeros((2 * kf, m1), np.float64)
    fcs[:nk] = np.cos(th)
    fcs[kf:kf + nk] = -np.sin(th)
    ck = np.full((nk,), 2.0 / nf)
    ck[0] = 1.0 / nf
    ck[-1] = 1.0 / nf
    ginv = np.zeros((m1, 2 * kf), np.float64)
    ginv[:, :nk] = (np.cos(th) * ck[:, None]).T
    ginv[:, kf:kf + nk] = (-np.sin(th) * ck[:, None]).T
    th1 = 2.0 * np.pi * np.arange(kp, dtype=np.float64) / nf
    th1[nk:] = 0.0
    cos_k = np.repeat(np.cos(th1)[:, None], LANES, axis=1)
    sin_k = np.repeat(np.sin(th1)[:, None], LANES, axis=1)
    thq = th1[None, :] * np.arange(NP, dtype=np.float64)[:, None] / NP
    twc = np.repeat(np.cos(thq)[:, :, None], LANES, axis=2)
    tws = np.repeat(np.sin(thq)[:, :, None], LANES, axis=2)
    return tuple(jnp.asarray(t, dtype=F32) for t in (fcs, ginv, cos_k, sin_k, twc, tws))


def _pick_tile(n, want):
    t = min(n, want)
    while n % t:
        t //= 2
    return t


def kernel(x, p, ffn1_norm_pre, ffn1_norm_post, ffn1_w_gate, ffn1_w_up, ffn1_w_down,
           mix_norm_pre, mix_norm_post, w_in, hy_short_w, hy_short_b,
           filt_w1, filt_b1, filt_freq1, filt_w2, filt_b2, filt_freq2, filt_w3,
           filt_deltas, hy_bias, q_norm, k_norm, w_hy_out, w_att_out, w_out,
           ffn2_norm_pre, ffn2_norm_post, ffn2_w_gate, ffn2_w_up, ffn2_w_down,
           ple_norm_pre, ple_norm_post, w_ple_gate, w_ple_proj):
    batch, seq, d = x.shape
    depth = p.shape[0]
    n = batch * seq
    nhy = 3 * HY_WIDTH
    nmix = nhy + ATT_Q + 2 * ATT_KV
    hy_blocks = HY_WIDTH // LANES

    tm = _pick_tile(seq, 512)
    tq = _pick_tile(seq, 512)
    cos_t, sin_t = _rope_tables(seq)
    bdq = _block_diag_mean(ATT_Q)
    bdk = _block_diag_mean(ATT_KV)
    ze = LANES
    zf = _filter_features(seq, ze)
    row = lambda a: a.reshape(1, -1).astype(F32)
    bf = lambda a: a.astype(BF16)
    fcs, ginv, cos_k, sin_k, twc, tws = _dft_tables(seq)
    fcs, ginv = bf(fcs), bf(ginv)

    x2d = x.reshape(n, d)
    for i in range(depth):
        x2d = _ffn(x2d, row(ffn1_norm_pre[i]), row(ffn1_norm_post[i]),
                   bf(ffn1_w_gate[i]), bf(ffn1_w_up[i]), bf(ffn1_w_down[i]), tm)

        hy, q, k, v = _inproj(
            x2d, row(mix_norm_pre[i]), bf(w_in[i][:, :nmix]),
            row(jnp.tile(q_norm[i], N_Q_HEADS)), row(jnp.tile(k_norm[i], N_KV_HEADS)),
            bdq, bdk, cos_t, sin_t, seq, tm)

        w3 = filt_w3[i].reshape(FILT_HID, HY_ORDER, 2, HY_WIDTH)
        dl = filt_deltas[i]
        w1p = jnp.pad(filt_w1[i], ((0, ze - FILT_EMB), (0, 0)))
        hf, hb = _filt_mlp(
            zf, w1p, row(filt_b1[i]), row(filt_freq1[i]), filt_w2[i],
            row(filt_b2[i]), row(filt_freq2[i]),
            w3[:, :, 0].reshape(FILT_HID, -1), w3[:, :, 1].reshape(FILT_HID, -1),
            row(dl[:, 0]), row(dl[:, 1]), LANES)
        hr, hi = _filt_dft(hf, hb, fcs, cos_k, sin_k, twc, tws)

        hy3 = hy.reshape(batch, seq, nhy)
        sw = hy_short_w[i]
        sb = row(hy_short_b[i])
        z1 = _longconv(hy3, 0, hy3, hy_blocks, sw, sb, 0, row(hy_bias[i][0]),
                       0, hr, hi, twc, tws, fcs, ginv, True)
        ya = _longconv(z1, 0, hy3, 2 * hy_blocks, sw, sb, 0, row(hy_bias[i][1]),
                       hy_blocks, hr, hi, twc, tws, fcs, ginv, False)

        yb = _attention(q, k, v, q_norm[i], k_norm[i], batch, seq, tq)

        x2d = _merge(x2d, ya.reshape(n, HY_WIDTH), yb, row(mix_norm_pre[i]),
                     row(mix_norm_post[i]), bf(w_in[i][:, nmix:]), bf(w_hy_out[i]),
                     bf(w_att_out[i]), bf(w_out[i]), tm)

        x2d = _ffn_ple(x2d, row(ffn2_norm_pre[i]), row(ffn2_norm_post[i]),
                       bf(ffn2_w_gate[i]), bf(ffn2_w_up[i]), bf(ffn2_w_down[i]),
                       p[i].reshape(n, -1), row(ple_norm_pre[i]), row(ple_norm_post[i]),
                       bf(w_ple_gate[i]), bf(w_ple_proj[i]), tm)
    return x2d.reshape(batch, seq, d)
```

```python
import functools
import math

import numpy as np
import jax
import jax.numpy as jnp
from jax import lax
from jax.experimental import pallas as pl
from jax.experimental.pallas import tpu as pltpu

F32 = jnp.float32
BF16 = jnp.bfloat16

GRID_W = 64
HY_WIDTH = 512
HY_ORDER = 2
FILT_EMB = 33
FILT_BANDS = (FILT_EMB - 1) // 2
FILT_HID = 64
N_Q_HEADS = 8
N_KV_HEADS = 2
GROUP = N_Q_HEADS // N_KV_HEADS
HEAD_DIM = 64
AXIS_DIM = HEAD_DIM // 2
ROPE_THETA = 10000.0
EPS = 1e-6
ATT_Q = N_Q_HEADS * HEAD_DIM
ATT_KV = N_KV_HEADS * HEAD_DIM

LANES = 128
SUBLANES = 8
VMEM_LIMIT_BYTES = 60 * 1024 * 1024

NP = 8


def _cparams(sem):
    return pltpu.CompilerParams(dimension_semantics=sem,
                                vmem_limit_bytes=VMEM_LIMIT_BYTES)


def _const_spec(shape):
    nd = len(shape)
    return pl.BlockSpec(shape, lambda *_: (0,) * nd, pipeline_mode=pl.Buffered(1))


def _rms(x, g):
    return x * lax.rsqrt(jnp.mean(x * x, axis=-1, keepdims=True) + EPS) * g


def _sigmoid(x):
    return 1.0 / (1.0 + jnp.exp(-x))


def _ffn_block(x, gpre_ref, gpost_ref, wg_ref, wu_ref, wd_ref):
    xn = _rms(x, gpre_ref[...]).astype(BF16)
    g = jnp.dot(xn, wg_ref[...], preferred_element_type=F32)
    u = jnp.dot(xn, wu_ref[...], preferred_element_type=F32)
    h = (g * _sigmoid(g) * u).astype(BF16)
    y = jnp.dot(h, wd_ref[...], preferred_element_type=F32)
    return x + 0.5 * _rms(y, gpost_ref[...])


def _ffn_kernel(x_ref, gpre_ref, gpost_ref, wg_ref, wu_ref, wd_ref, o_ref):
    o_ref[...] = _ffn_block(x_ref[...], gpre_ref, gpost_ref, wg_ref, wu_ref, wd_ref)


def _ffn(x2d, gpre, gpost, wg, wu, wd, tm):
    n, d = x2d.shape
    ff = wg.shape[1]
    return pl.pallas_call(
        _ffn_kernel,
        grid=(n // tm,),
        in_specs=[
            pl.BlockSpec((tm, d), lambda i: (i, 0)),
            _const_spec((1, d)), _const_spec((1, d)),
            _const_spec((d, ff)), _const_spec((d, ff)), _const_spec((ff, d)),
        ],
        out_specs=pl.BlockSpec((tm, d), lambda i: (i, 0)),
        out_shape=jax.ShapeDtypeStruct((n, d), F32),
        compiler_params=_cparams(("parallel",)),
        name="ffn",
    )(x2d, gpre, gpost, wg, wu, wd)


Q_SCALE = HEAD_DIM ** -0.5 * math.log2(math.e)


def _head_norm_rope(t, gain, bd, cos, sin):
    ms = jnp.dot((t * t).astype(BF16), bd, preferred_element_type=F32)
    tn = t * lax.rsqrt(ms + EPS) * gain
    w = t.shape[-1]
    lane = lax.broadcasted_iota(jnp.int32, tn.shape, 1)
    nxt = pltpu.roll(tn, w - 1, 1)
    prv = pltpu.roll(tn, 1, 1)
    swapped = jnp.where((lane & 1) == 0, nxt, prv)
    return tn * cos + swapped * sin


def _inproj_kernel(x_ref, gpre_ref, w_ref, qg_ref, kg_ref, bdq_ref, bdk_ref,
                   cos_ref, sin_ref, hy_ref, q_ref, k_ref, v_ref):
    nhy = hy_ref.shape[-1]
    u = _rms(x_ref[...], gpre_ref[...]).astype(BF16)
    proj = jnp.dot(u, w_ref[...], preferred_element_type=F32)
    hy_ref[...] = proj[:, :nhy]
    q = proj[:, nhy:nhy + ATT_Q]
    k = proj[:, nhy + ATT_Q:nhy + ATT_Q + ATT_KV]
    v = proj[:, nhy + ATT_Q + ATT_KV:]
    cos_k = cos_ref[...]
    sin_k = sin_ref[...]
    reps = ATT_Q // ATT_KV
    cos_q = jnp.concatenate([cos_k] * reps, axis=1)
    sin_q = jnp.concatenate([sin_k] * reps, axis=1)
    qr = _head_norm_rope(q, qg_ref[...], bdq_ref[...], cos_q, sin_q)
    kr = _head_norm_rope(k, kg_ref[...], bdk_ref[...], cos_k, sin_k)
    q_ref[...] = (qr * Q_SCALE).astype(BF16)
    k_ref[...] = kr.astype(BF16)
    v_ref[...] = v.astype(BF16)


def _inproj(x2d, gpre, w, qg, kg, bdq, bdk, cos_t, sin_t, seq, tm):
    n, d = x2d.shape
    ncols = w.shape[1]
    nhy = ncols - ATT_Q - 2 * ATT_KV
    tiles_per_seq = seq // tm
    return pl.pallas_call(
        _inproj_kernel,
        grid=(n // tm,),
        in_specs=[
            pl.BlockSpec((tm, d), lambda i: (i, 0)),
            _const_spec((1, d)),
            _const_spec((d, ncols)),
            _const_spec((1, ATT_Q)), _const_spec((1, ATT_KV)),
            _const_spec((ATT_Q, ATT_Q)), _const_spec((ATT_KV, ATT_KV)),
            pl.BlockSpec((tm, ATT_KV), lambda i: (i % tiles_per_seq, 0)),
            pl.BlockSpec((tm, ATT_KV), lambda i: (i % tiles_per_seq, 0)),
        ],
        out_specs=[
            pl.BlockSpec((tm, nhy), lambda i: (i, 0)),
            pl.BlockSpec((tm, ATT_Q), lambda i: (i, 0)),
            pl.BlockSpec((tm, ATT_KV), lambda i: (i, 0)),
            pl.BlockSpec((tm, ATT_KV), lambda i: (i, 0)),
        ],
        out_shape=[
            jax.ShapeDtypeStruct((n, nhy), F32),
            jax.ShapeDtypeStruct((n, ATT_Q), BF16),
            jax.ShapeDtypeStruct((n, ATT_KV), BF16),
            jax.ShapeDtypeStruct((n, ATT_KV), BF16),
        ],
        compiler_params=_cparams(("parallel",)),
        name="inproj",
    )(x2d, gpre, w, qg, kg, bdq, bdk, cos_t, sin_t)


SHIFT_BOUND_MAX = 60.0
BOUND_SLACK = 1.02


BF16_SUBLANES = 2 * SUBLANES
V_ROWS = HEAD_DIM + BF16_SUBLANES


def _attn_shift_kernel(shift_ref, q_ref, k_ref, v_ref, o_ref, vt_ref):
    gw = GROUP * HEAD_DIM
    tq = q_ref.shape[0]
    seq = k_ref.shape[0]

    @pl.when(pl.program_id(1) == 0)
    def _():
        vt = v_ref[...].astype(F32).T
        row = lax.broadcasted_iota(jnp.int32, (V_ROWS - HEAD_DIM, seq), 0)
        ones = jnp.where(row == 0, 1.0, 0.0)
        for g in range(N_KV_HEADS):
            vg = vt[g * HEAD_DIM:(g + 1) * HEAD_DIM]
            vt_ref[g] = jnp.concatenate([vg, ones], axis=0).astype(BF16)

    kk = k_ref[...]
    shift = shift_ref[0:1, 0:1]
    lane = lax.broadcasted_iota(jnp.int32, (tq, ATT_KV), 1)
    for g in range(N_KV_HEADS):
        qg = q_ref[:, g * gw:(g + 1) * gw].astype(F32)
        lo = g * HEAD_DIM
        outs = []
        for h in range(GROUP):
            sh = (lo - h * HEAD_DIM) % gw
            qh = (qg if sh == 0 else pltpu.roll(qg, sh, 1))[:, :ATT_KV]
            qsel = jnp.where((lane >= lo) & (lane < lo + HEAD_DIM), qh, 0.0).astype(BF16)
            st = lax.dot_general(kk, qsel, (((1,), (1,)), ((), ())),
                                 preferred_element_type=F32)
            p = jnp.exp2(st - shift).astype(BF16)
            ot = jnp.dot(vt_ref[g], p, preferred_element_type=F32)
            outs.append(ot[:HEAD_DIM] * (1.0 / ot[HEAD_DIM:HEAD_DIM + 1]))
        og = jnp.concatenate(outs, axis=0)
        o_ref[:, g * gw:(g + 1) * gw] = og.T.astype(BF16)


def _attn_max_kernel(q_ref, k_ref, v_ref, o_ref, k0_ref, v0_ref):
    gw = GROUP * HEAD_DIM
    tq = q_ref.shape[0]

    @pl.when(pl.program_id(1) == 0)
    def _():
        lane = lax.broadcasted_iota(jnp.int32, (k_ref.shape[0], gw), 1)
        reps = gw // ATT_KV
        kk = jnp.concatenate([k_ref[...].astype(F32)] * reps, axis=1)
        vv = jnp.concatenate([v_ref[...].astype(F32)] * reps, axis=1)
        for g in range(N_KV_HEADS):
            ks = kk if g == 0 else pltpu.roll(kk, gw - g * HEAD_DIM, 1)
            vs = vv if g == 0 else pltpu.roll(vv, gw - g * HEAD_DIM, 1)
            k0_ref[g] = jnp.where(lane < HEAD_DIM, ks, 0.0).astype(BF16)
            v0_ref[g] = jnp.where(lane < HEAD_DIM, vs,
                                  jnp.where(lane == HEAD_DIM, 1.0, 0.0)).astype(BF16)

    lane_o = lax.broadcasted_iota(jnp.int32, (tq, gw), 1)
    for g in range(N_KV_HEADS):
        qg = q_ref[:, g * gw:(g + 1) * gw].astype(F32)
        acc = jnp.zeros((tq, gw), F32)
        for h in range(GROUP):
            qh = qg if h == 0 else pltpu.roll(qg, gw - h * HEAD_DIM, 1)
            s = lax.dot_general(qh.astype(BF16), k0_ref[g], (((1,), (1,)), ((), ())),
                                preferred_element_type=F32)
            m = jnp.max(s, axis=-1, keepdims=True)
            p = jnp.exp2(s - m)
            o = jnp.dot(p.astype(BF16), v0_ref[g], preferred_element_type=F32)
            l = o[:, HEAD_DIM:HEAD_DIM + 1]
            o = jnp.where(lane_o < HEAD_DIM, o * (1.0 / l), 0.0)
            acc = acc + (o if h == 0 else pltpu.roll(o, h * HEAD_DIM, 1))
        o_ref[:, g * gw:(g + 1) * gw] = acc.astype(BF16)


def _attention_call(shift, q, k, v, *, batch, seq, tq, use_shift):
    n = q.shape[0]
    gw = GROUP * HEAD_DIM
    qt = seq // tq
    qkv_specs = [
        pl.BlockSpec((tq, ATT_Q), lambda b, i: (b * qt + i, 0)),
        pl.BlockSpec((seq, ATT_KV), lambda b, i: (b, 0)),
        pl.BlockSpec((seq, ATT_KV), lambda b, i: (b, 0)),
    ]
    common = dict(
        grid=(batch, qt),
        out_specs=pl.BlockSpec((tq, ATT_Q), lambda b, i: (b * qt + i, 0)),
        out_shape=jax.ShapeDtypeStruct((n, ATT_Q), BF16),
        compiler_params=_cparams(("arbitrary", "arbitrary")),
    )
    if use_shift:
        return pl.pallas_call(
            _attn_shift_kernel,
            in_specs=[_const_spec(shift.shape)] + qkv_specs,
            scratch_shapes=[pltpu.VMEM((N_KV_HEADS, V_ROWS, seq), BF16)],
            name="attention_shift", **common,
        )(shift, q, k, v)
    return pl.pallas_call(
        _attn_max_kernel,
        in_specs=qkv_specs,
        scratch_shapes=[pltpu.VMEM((N_KV_HEADS, seq, gw), BF16),
                        pltpu.VMEM((N_KV_HEADS, seq, gw), BF16)],
        name="attention_max", **common,
    )(q, k, v)


def _attention(q, k, v, q_gain, k_gain, batch, seq, tq):
    bound = ((HEAD_DIM * Q_SCALE * BOUND_SLACK) * jnp.max(jnp.abs(q_gain))
             * jnp.max(jnp.abs(k_gain))).astype(F32)
    shift = jnp.full((SUBLANES, LANES), bound, F32)
    call = functools.partial(_attention_call, batch=batch, seq=seq, tq=tq)
    return lax.cond(bound < SHIFT_BOUND_MAX,
                    functools.partial(call, use_shift=True),
                    functools.partial(call, use_shift=False),
                    shift, q, k, v)


def _filt_mlp_kernel(z_ref, w1_ref, b1_ref, f1_ref, w2_ref, b2_ref, f2_ref,
                     w3f_ref, w3b_ref, df_ref, db_ref, hf_ref, hb_ref, h_ref):
    hp = lax.Precision.HIGHEST

    @pl.when(pl.program_id(0) == 0)
    def _():
        h1 = jnp.sin(f1_ref[...] * (jnp.dot(z_ref[...], w1_ref[...], precision=hp,
                                            preferred_element_type=F32) + b1_ref[...]))
        h_ref[...] = jnp.sin(f2_ref[...] * (jnp.dot(h1, w2_ref[...], precision=hp,
                                                    preferred_element_type=F32)
                                            + b2_ref[...]))

    h = h_ref[...]
    t = z_ref[:, 0:1]
    hf = jnp.dot(h, w3f_ref[...], precision=hp, preferred_element_type=F32)
    hb = jnp.dot(h, w3b_ref[...], precision=hp, preferred_element_type=F32)
    hf = hf * jnp.exp(-t * jnp.abs(df_ref[...]))
    hb = hb * jnp.exp(-t * jnp.abs(db_ref[...]))
    row = lax.broadcasted_iota(jnp.int32, hf.shape, 0)
    both = jnp.abs(hf) + jnp.abs(hb)
    merged = jnp.abs(hf + hb)
    s = jnp.sum(jnp.where(row == 0, merged, both), axis=0, keepdims=True)
    inv = 1.0 / s
    hf_ref[...] = hf * inv
    hb_ref[...] = hb * inv


def _filt_mlp(zf, w1, b1, f1, w2, b2, f2, w3f, w3b, df, db, tc):
    seq, ze = zf.shape
    ncol = w3f.shape[1]
    hid = w2.shape[0]
    col = lambda j: (0, j)
    return pl.pallas_call(
        _filt_mlp_kernel,
        grid=(ncol // tc,),
        in_specs=[
            _const_spec((seq, ze)), _const_spec((ze, hid)), _const_spec((1, hid)),
            _const_spec((1, hid)), _const_spec((hid, hid)), _const_spec((1, hid)),
            _const_spec((1, hid)),
            pl.BlockSpec((hid, tc), col), pl.BlockSpec((hid, tc), col),
            pl.BlockSpec((1, tc), col), pl.BlockSpec((1, tc), col),
        ],
        out_specs=[pl.BlockSpec((seq, tc), col), pl.BlockSpec((seq, tc), col)],
        out_shape=[jax.ShapeDtypeStruct((seq, ncol), F32),
                   jax.ShapeDtypeStruct((seq, ncol), F32)],
        scratch_shapes=[pltpu.VMEM((seq, hid), F32)],
        compiler_params=_cparams(("arbitrary",)),
        name="filt_mlp",
    )(zf, w1, b1, f1, w2, b2, f2, w3f, w3b, df, db)


def _phases(ref, rows):
    return [ref[pl.ds(q, rows, stride=NP), :] for q in range(NP)]


def _split_bf16(x):
    hi = x.astype(BF16)
    lo = (x - hi.astype(F32)).astype(BF16)
    return hi, lo


def _cmul_tw(xr, xi, c, s, conj):
    if conj:
        return xr * c - xi * s, xi * c + xr * s
    return xr * c + xi * s, xi * c - xr * s


def _fft_list(xs, sign):
    n = len(xs)
    if n == 1:
        return xs
    even = _fft_list(xs[0::2], sign)
    odd = _fft_list(xs[1::2], sign)
    half = n // 2
    out = [None] * n
    for k in range(half):
        er, ei = even[k]
        o_r, o_i = odd[k]
        if k == 0:
            tr, ti = o_r, o_i
            out[k] = (er + tr, ei + ti)
            out[k + half] = (er - tr, ei - ti)
        elif 4 * k == n:
            if sign > 0:
                out[k] = (er - o_i, ei + o_r)
                out[k + half] = (er + o_i, ei - o_r)
            else:
                out[k] = (er + o_i, ei - o_r)
                out[k + half] = (er - o_i, ei + o_r)
        else:
            c = math.cos(2.0 * math.pi * k / n)
            s = sign * math.sin(2.0 * math.pi * k / n)
            tr = o_r * c - o_i * s
            ti = o_r * s + o_i * c
            out[k] = (er + tr, ei + ti)
            out[k + half] = (er - tr, ei - ti)
    return out


def _filt_dft_kernel(hf_ref, hb_ref, fcs_ref, cos_ref, sin_ref, twc_ref, tws_ref,
                     hr_ref, hi_ref):
    rows = hf_ref.shape[0] // NP
    kp = hr_ref.shape[1]
    kf = fcs_ref.shape[0] // 2
    xs = _phases(hf_ref, rows) + _phases(hb_ref, rows)
    xcat = jnp.concatenate(xs, axis=1)
    hi, lo = _split_bf16(xcat)
    fcs = fcs_ref[...]
    a = (jnp.dot(fcs, hi, preferred_element_type=F32)
         + jnp.dot(fcs, lo, preferred_element_type=F32))
    c = cos_ref[...]
    s = sin_ref[...]

    def part(dirn, ph):
        j = (dirn * NP + ph) * LANES
        return a[:kp, j:j + LANES], a[kf:kf + kp, j:j + LANES]

    fr0, fi0 = part(0, 0)
    br0, bi0 = part(1, 0)
    gs = [(fr0 + br0, fi0 - bi0)]
    for d in range(1, NP):
        fr, fi = part(0, d)
        br, bi = part(1, NP - d)
        gr = fr + c * br + s * bi
        gi = fi + s * br - c * bi
        gs.append(_cmul_tw(gr, gi, twc_ref[d], tws_ref[d], False))
    hs = _fft_list(gs, -1)
    for k2 in range(NP):
        hr_ref[k2] = hs[k2][0] * (1.0 / NP)
        hi_ref[k2] = hs[k2][1] * (1.0 / NP)


def _filt_dft(hf, hb, fcs, cos_k, sin_k, twc, tws):
    seq, ncol = hf.shape
    kp = cos_k.shape[0]
    col = lambda j: (0, j)
    out_spec = pl.BlockSpec((NP, kp, LANES), lambda j: (0, 0, j))
    return pl.pallas_call(
        _filt_dft_kernel,
        grid=(ncol // LANES,),
        in_specs=[pl.BlockSpec((seq, LANES), col), pl.BlockSpec((seq, LANES), col),
                  _const_spec(fcs.shape), _const_spec(cos_k.shape),
                  _const_spec(sin_k.shape), _const_spec(twc.shape),
                  _const_spec(tws.shape)],
        out_specs=[out_spec, out_spec],
        out_shape=[jax.ShapeDtypeStruct((NP, kp, ncol), F32),
                   jax.ShapeDtypeStruct((NP, kp, ncol), F32)],
        compiler_params=_cparams(("parallel",)),
        name="filt_dft",
    )(hf, hb, fcs, cos_k, sin_k, twc, tws)


def _short_conv_phases(raw, w_ref, b_ref):
    rows = raw[0].shape[0]
    row = lax.broadcasted_iota(jnp.int32, raw[0].shape, 0)
    before = jnp.where(row == 0, 0.0, pltpu.roll(raw[NP - 1], 1, 0))
    after = jnp.where(row == rows - 1, 0.0, pltpu.roll(raw[0], rows - 1, 0))
    w0 = w_ref[0:1, :]
    w1 = w_ref[1:2, :]
    w2 = w_ref[2:3, :]
    b = b_ref[...]
    out = []
    for q in range(NP):
        prev = raw[q - 1] if q > 0 else before
        nxt = raw[q + 1] if q < NP - 1 else after
        out.append(prev * w0 + raw[q] * w1 + nxt * w2 + b)
    return out


def _longconv_kernel(x_ref, g_ref, wx_ref, bx_ref, wg_ref, bg_ref, bias_ref,
                     hr_ref, hi_ref, twc_ref, tws_ref, fcs_ref, ginv_ref, o_ref,
                     a_ref, b_ref, *, conv_x):
    rows = x_ref.shape[0] // NP
    kp = hr_ref.shape[1]
    kf = fcs_ref.shape[0] // 2
    xs = _phases(x_ref, rows)
    if conv_x:
        xs = _short_conv_phases(xs, wx_ref, bx_ref)
    gs = _short_conv_phases(_phases(g_ref, rows), wg_ref, bg_ref)

    xcat = jnp.concatenate([x.astype(BF16) for x in xs], axis=1)
    a_ref[...] = jnp.dot(fcs_ref[...], xcat, preferred_element_type=F32)

    def body(i):
        r0 = i * SUBLANES
        rs = pl.ds(r0, SUBLANES)
        zs = []
        for q in range(NP):
            ar = a_ref[rs, q * LANES:(q + 1) * LANES]
            ai = a_ref[pl.ds(kf + r0, SUBLANES), q * LANES:(q + 1) * LANES]
            zs.append((ar, ai) if q == 0 else
                      _cmul_tw(ar, ai, twc_ref[q, rs, :], tws_ref[q, rs, :], False))
        zs = _fft_list(zs, -1)
        ys = []
        for k2 in range(NP):
            zr, zi = zs[k2]
            hr = hr_ref[k2, rs, :]
            hi = hi_ref[k2, rs, :]
            ys.append((hr * zr - hi * zi, hr * zi + hi * zr))
        ys = _fft_list(ys, 1)
        for p in range(NP):
            br, bi = ys[p]
            if p > 0:
                br, bi = _cmul_tw(br, bi, twc_ref[p, rs, :], tws_ref[p, rs, :], True)
            b_ref[rs, p * LANES:(p + 1) * LANES] = br
            b_ref[pl.ds(kf + r0, SUBLANES), p * LANES:(p + 1) * LANES] = bi

    for i in range(kp // SUBLANES):
        body(i)
    if kf > kp:
        zeros = jnp.zeros((kf - kp, NP * LANES), F32)
        b_ref[kp:kf, :] = zeros
        b_ref[kf + kp:, :] = zeros

    y = jnp.dot(ginv_ref[...], b_ref[...].astype(BF16), preferred_element_type=F32)
    bias = bias_ref[...]
    for p in range(NP):
        yp = y[:, p * LANES:(p + 1) * LANES]
        o_ref[pl.ds(p, rows, stride=NP), :] = gs[p] * (yp + xs[p] * bias)


def _longconv(x3, x_blk0, hy3, g_blk0, sw, sb, wx_blk0, bias, h_blk0, hr, hi,
              twc, tws, fcs, ginv, conv_x):
    batch, seq, _ = hy3.shape
    nch = bias.shape[1] // LANES
    kp = hr.shape[1]
    kernel = functools.partial(_longconv_kernel, conv_x=conv_x)
    return pl.pallas_call(
        kernel,
        grid=(nch, batch),
        in_specs=[
            pl.BlockSpec((None, seq, LANES), lambda c, b: (b, 0, x_blk0 + c)),
            pl.BlockSpec((None, seq, LANES), lambda c, b: (b, 0, g_blk0 + c)),
            pl.BlockSpec((3, LANES), lambda c, b: (0, wx_blk0 + c)),
            pl.BlockSpec((1, LANES), lambda c, b: (0, wx_blk0 + c)),
            pl.BlockSpec((3, LANES), lambda c, b: (0, g_blk0 + c)),
            pl.BlockSpec((1, LANES), lambda c, b: (0, g_blk0 + c)),
            pl.BlockSpec((1, LANES), lambda c, b: (0, c)),
            pl.BlockSpec((NP, kp, LANES), lambda c, b: (0, 0, h_blk0 + c),
                         pipeline_mode=pl.Buffered(1)),
            pl.BlockSpec((NP, kp, LANES), lambda c, b: (0, 0, h_blk0 + c),
                         pipeline_mode=pl.Buffered(1)),
            _const_spec(twc.shape), _const_spec(tws.shape),
            _const_spec(fcs.shape), _const_spec(ginv.shape),
        ],
        out_specs=pl.BlockSpec((None, seq, LANES), lambda c, b: (b, 0, c)),
        out_shape=jax.ShapeDtypeStruct((batch, seq, nch * LANES), F32),
        scratch_shapes=[pltpu.VMEM((fcs.shape[0], NP * LANES), F32),
                        pltpu.VMEM((fcs.shape[0], NP * LANES), F32)],
        compiler_params=_cparams(("arbitrary", "arbitrary")),
        name="longconv_x" if conv_x else "longconv",
    )(x3, hy3, sw, sb, sw, sb, bias, hr, hi, twc, tws, fcs, ginv)


def _merge_kernel(x_ref, ya_ref, yb_ref, gpre_ref, gpost_ref, wgate_ref,
                  wa_ref, wb_ref, wo_ref, o_ref):
    x = x_ref[...]
    d = x.shape[-1]
    u = _rms(x, gpre_ref[...]).astype(BF16)
    gates = jnp.dot(u, wgate_ref[...], preferred_element_type=F32)
    ma = jnp.dot(ya_ref[...].astype(BF16), wa_ref[...], preferred_element_type=F32)
    mb = jnp.dot(yb_ref[...], wb_ref[...], preferred_element_type=F32)
    m = _sigmoid(gates[:, :d]) * ma + _sigmoid(gates[:, d:]) * mb
    y = jnp.dot(m.astype(BF16), wo_ref[...], preferred_element_type=F32)
    o_ref[...] = x + _rms(y, gpost_ref[...])


def _merge(x2d, ya, yb, gpre, gpost, wgate, wa, wb, wo, tm):
    n, d = x2d.shape
    row = lambda i: (i, 0)
    return pl.pallas_call(
        _merge_kernel,
        grid=(n // tm,),
        in_specs=[
            pl.BlockSpec((tm, d), row),
            pl.BlockSpec((tm, ya.shape[1]), row),
            pl.BlockSpec((tm, yb.shape[1]), row),
            _const_spec((1, d)), _const_spec((1, d)),
            _const_spec(wgate.shape), _const_spec(wa.shape),
            _const_spec(wb.shape), _const_spec(wo.shape),
        ],
        out_specs=pl.BlockSpec((tm, d), row),
        out_shape=jax.ShapeDtypeStruct((n, d), F32),
        compiler_params=_cparams(("parallel",)),
        name="merge",
    )(x2d, ya, yb, gpre, gpost, wgate, wa, wb, wo)


def _ffn_ple_kernel(x_ref, gpre_ref, gpost_ref, wg_ref, wu_ref, wd_ref,
                    p_ref, ppre_ref, ppost_ref, wpg_ref, wpp_ref, o_ref):
    x1 = _ffn_block(x_ref[...], gpre_ref, gpost_ref, wg_ref, wu_ref, wd_ref)
    u = _rms(x1, ppre_ref[...]).astype(BF16)
    g = _sigmoid(jnp.dot(u, wpg_ref[...], preferred_element_type=F32))
    e = jnp.dot(p_ref[...].astype(BF16), wpp_ref[...], preferred_element_type=F32)
    o_ref[...] = x1 + _rms(g * e, ppost_ref[...])


def _ffn_ple(x2d, gpre, gpost, wg, wu, wd, p2d, ppre, ppost, wpg, wpp, tm):
    n, d = x2d.shape
    row = lambda i: (i, 0)
    return pl.pallas_call(
        _ffn_ple_kernel,
        grid=(n // tm,),
        in_specs=[
            pl.BlockSpec((tm, d), row),
            _const_spec((1, d)), _const_spec((1, d)),
            _const_spec(wg.shape), _const_spec(wu.shape), _const_spec(wd.shape),
            pl.BlockSpec((tm, p2d.shape[1]), row),
            _const_spec((1, d)), _const_spec((1, d)),
            _const_spec(wpg.shape), _const_spec(wpp.shape),
        ],
        out_specs=pl.BlockSpec((tm, d), row),
        out_shape=jax.ShapeDtypeStruct((n, d), F32),
        compiler_params=_cparams(("parallel",)),
        name="ffn_ple",
    )(x2d, gpre, gpost, wg, wu, wd, p2d, ppre, ppost, wpg, wpp)


def _rope_tables(seq):
    rows = seq // GRID_W
    row = np.repeat(np.arange(rows, dtype=np.float32), GRID_W)
    col = np.tile(np.arange(GRID_W, dtype=np.float32), rows)
    inv = (ROPE_THETA ** (-np.arange(0, AXIS_DIM, 2, dtype=np.float32) / AXIS_DIM)
           ).astype(np.float32)
    ang = np.concatenate([row[:, None] * inv, col[:, None] * inv], axis=-1)
    cos = np.repeat(np.cos(ang), 2, axis=1)
    sin = np.repeat(np.sin(ang), 2, axis=1)
    sign = np.tile(np.array([-1.0, 1.0], np.float32), HEAD_DIM // 2)
    cos_t = np.tile(cos, (1, N_KV_HEADS)).astype(np.float32)
    sin_t = np.tile(sin * sign, (1, N_KV_HEADS)).astype(np.float32)
    return jnp.asarray(cos_t), jnp.asarray(sin_t)


def _block_diag_mean(width):
    idx = np.arange(width) // HEAD_DIM
    return jnp.asarray((idx[:, None] == idx[None, :]).astype(np.float32) / HEAD_DIM,
                       dtype=BF16)


def _filter_features(seq, pad_to):
    t = np.linspace(0.0, 1.0, seq, dtype=np.float32)[:, None]
    w = ((2.0 * math.pi / seq) * np.arange(seq, dtype=np.float32)).astype(np.float32)
    bands = np.linspace(1e-4, FILT_BANDS - 1, FILT_BANDS, dtype=np.float32)
    ang = w[:, None] * bands[None, :]
    z = np.concatenate([t, np.cos(ang), -np.sin(ang)], axis=-1).astype(np.float32)
    return jnp.asarray(np.pad(z, ((0, 0), (0, pad_to - z.shape[1]))))


def _dft_tables(seq):
    m1 = seq // NP
    nf = 2 * m1
    nk = nf // 2 + 1
    kp = -(-nk // SUBLANES) * SUBLANES
    half_tile = LANES // 2
    kf = -(-nk // half_tile) * half_tile
    k = np.arange(nk, dtype=np.float64)[:, None]
    m = np.arange(m1, dtype=np.float64)[None, :]
    th = 2.0 * np.pi * k * m / nf
    fcs = np.zeros((2 * kf, m1), np.float64)
    fcs[:nk] = np.cos(th)
    fcs[kf:kf + nk] = -np.sin(th)
    ck = np.full((nk,), 2.0 / nf)
    ck[0] = 1.0 / nf
    ck[-1] = 1.0 / nf
    ginv = np.zeros((m1, 2 * kf), np.float64)
    ginv[:, :nk] = (np.cos(th) * ck[:, None]).T
    ginv[:, kf:kf + nk] = (-np.sin(th) * ck[:, None]).T
    th1 = 2.0 * np.pi * np.arange(kp, dtype=np.float64) / nf
    th1[nk:] = 0.0
    cos_k = np.repeat(np.cos(th1)[:, None], LANES, axis=1)
    sin_k = np.repeat(np.sin(th1)[:, None], LANES, axis=1)
    thq = th1[None, :] * np.arange(NP, dtype=np.float64)[:, None] / NP
    twc = np.repeat(np.cos(thq)[:, :, None], LANES, axis=2)
    tws = np.repeat(np.sin(thq)[:, :, None], LANES, axis=2)
    return tuple(jnp.asarray(t, dtype=F32) for t in (fcs, ginv, cos_k, sin_k, twc, tws))


def _pick_tile(n, want):
    t = min(n, want)
    while n % t:
        t //= 2
    return t


def kernel(x, p, ffn1_norm_pre, ffn1_norm_post, ffn1_w_gate, ffn1_w_up, ffn1_w_down,
           mix_norm_pre, mix_norm_post, w_in, hy_short_w, hy_short_b,
           filt_w1, filt_b1, filt_freq1, filt_w2, filt_b2, filt_freq2, filt_w3,
           filt_deltas, hy_bias, q_norm, k_norm, w_hy_out, w_att_out, w_out,
           ffn2_norm_pre, ffn2_norm_post, ffn2_w_gate, ffn2_w_up, ffn2_w_down,
           ple_norm_pre, ple_norm_post, w_ple_gate, w_ple_proj):
    batch, seq, d = x.shape
    depth = p.shape[0]
    n = batch * seq
    nhy = 3 * HY_WIDTH
    nmix = nhy + ATT_Q + 2 * ATT_KV
    hy_blocks = HY_WIDTH // LANES

    tm = _pick_tile(seq, 512)
    tq = _pick_tile(seq, 512)
    cos_t, sin_t = _rope_tables(seq)
    bdq = _block_diag_mean(ATT_Q)
    bdk = _block_diag_mean(ATT_KV)
    ze = LANES
    zf = _filter_features(seq, ze)
    row = lambda a: a.reshape(1, -1).astype(F32)
    bf = lambda a: a.astype(BF16)
    fcs, ginv, cos_k, sin_k, twc, tws = _dft_tables(seq)
    fcs, ginv = bf(fcs), bf(ginv)

    x2d = x.reshape(n, d)
    for i in range(depth):
        x2d = _ffn(x2d, row(ffn1_norm_pre[i]), row(ffn1_norm_post[i]),
                   bf(ffn1_w_gate[i]), bf(ffn1_w_up[i]), bf(ffn1_w_down[i]), tm)

        hy, q, k, v = _inproj(
            x2d, row(mix_norm_pre[i]), bf(w_in[i][:, :nmix]),
            row(jnp.tile(q_norm[i], N_Q_HEADS)), row(jnp.tile(k_norm[i], N_KV_HEADS)),
            bdq, bdk, cos_t, sin_t, seq, tm)

        w3 = filt_w3[i].reshape(FILT_HID, HY_ORDER, 2, HY_WIDTH)
        dl = filt_deltas[i]
        w1p = jnp.pad(filt_w1[i], ((0, ze - FILT_EMB), (0, 0)))
        hf, hb = _filt_mlp(
            zf, w1p, row(filt_b1[i]), row(filt_freq1[i]), filt_w2[i],
            row(filt_b2[i]), row(filt_freq2[i]),
            w3[:, :, 0].reshape(FILT_HID, -1), w3[:, :, 1].reshape(FILT_HID, -1),
            row(dl[:, 0]), row(dl[:, 1]), LANES)
        hr, hi = _filt_dft(hf, hb, fcs, cos_k, sin_k, twc, tws)

        hy3 = hy.reshape(batch, seq, nhy)
        sw = hy_short_w[i]
        sb = row(hy_short_b[i])
        z1 = _longconv(hy3, 0, hy3, hy_blocks, sw, sb, 0, row(hy_bias[i][0]),
                       0, hr, hi, twc, tws, fcs, ginv, True)
        ya = _longconv(z1, 0, hy3, 2 * hy_blocks, sw, sb, 0, row(hy_bias[i][1]),
                       hy_blocks, hr, hi, twc, tws, fcs, ginv, False)

        yb = _attention(q, k, v, q_norm[i], k_norm[i], batch, seq, tq)

        x2d = _merge(x2d, ya.reshape(n, HY_WIDTH), yb, row(mix_norm_pre[i]),
                     row(mix_norm_post[i]), bf(w_in[i][:, nmix:]), bf(w_hy_out[i]),
                     bf(w_att_out[i]), bf(w_out[i]), tm)

        x2d = _ffn_ple(x2d, row(ffn2_norm_pre[i]), row(ffn2_norm_post[i]),
                       bf(ffn2_w_gate[i]), bf(ffn2_w_up[i]), bf(ffn2_w_down[i]),
                       p[i].reshape(n, -1), row(ple_norm_pre[i]), row(ple_norm_post[i]),
                       bf(w_ple_gate[i]), bf(w_ple_proj[i]), tm)
    return x2d.reshape(batch, seq, d)
```

```python
import functools
import math

import numpy as np
import jax
import jax.numpy as jnp
from jax import lax
from jax.experimental import pallas as pl
from jax.experimental.pallas import tpu as pltpu

F32 = jnp.float32
BF16 = jnp.bfloat16

GRID_W = 64
HY_WIDTH = 512
HY_ORDER = 2
FILT_EMB = 33
FILT_BANDS = (FILT_EMB - 1) // 2
FILT_HID = 64
N_Q_HEADS = 8
N_KV_HEADS = 2
GROUP = N_Q_HEADS // N_KV_HEADS
HEAD_DIM = 64
AXIS_DIM = HEAD_DIM // 2
ROPE_THETA = 10000.0
EPS = 1e-6
ATT_Q = N_Q_HEADS * HEAD_DIM
ATT_KV = N_KV_HEADS * HEAD_DIM

LANES = 128
SUBLANES = 8
VMEM_LIMIT_BYTES = 60 * 1024 * 1024

NP = 8


def _cparams(sem):
    return pltpu.CompilerParams(dimension_semantics=sem,
                                vmem_limit_bytes=VMEM_LIMIT_BYTES)


def _const_spec(shape):
    nd = len(shape)
    return pl.BlockSpec(shape, lambda *_: (0,) * nd, pipeline_mode=pl.Buffered(1))


def _rms(x, g):
    return x * lax.rsqrt(jnp.mean(x * x, axis=-1, keepdims=True) + EPS) * g


def _sigmoid(x):
    return 1.0 / (1.0 + jnp.exp(-x))


def _ffn_block(x, gpre_ref, gpost_ref, wg_ref, wu_ref, wd_ref):
    xn = _rms(x, gpre_ref[...]).astype(BF16)
    g = jnp.dot(xn, wg_ref[...], preferred_element_type=F32)
    u = jnp.dot(xn, wu_ref[...], preferred_element_type=F32)
    h = (g * _sigmoid(g) * u).astype(BF16)
    y = jnp.dot(h, wd_ref[...], preferred_element_type=F32)
    return x + 0.5 * _rms(y, gpost_ref[...])


def _row_halves(ref):
    half = ref.shape[0] // 2
    return (slice(0, half), slice(half, 2 * half))


def _ffn_kernel(x_ref, gpre_ref, gpost_ref, wg_ref, wu_ref, wd_ref, o_ref):
    for rows in _row_halves(x_ref):
        o_ref[rows, :] = _ffn_block(x_ref[rows, :], gpre_ref, gpost_ref,
                                    wg_ref, wu_ref, wd_ref)


def _ffn(x2d, gpre, gpost, wg, wu, wd, tm):
    n, d = x2d.shape
    ff = wg.shape[1]
    return pl.pallas_call(
        _ffn_kernel,
        grid=(n // tm,),
        in_specs=[
            pl.BlockSpec((tm, d), lambda i: (i, 0)),
            _const_spec((1, d)), _const_spec((1, d)),
            _const_spec((d, ff)), _const_spec((d, ff)), _const_spec((ff, d)),
        ],
        out_specs=pl.BlockSpec((tm, d), lambda i: (i, 0)),
        out_shape=jax.ShapeDtypeStruct((n, d), F32),
        compiler_params=_cparams(("parallel",)),
        name="ffn",
    )(x2d, gpre, gpost, wg, wu, wd)


Q_SCALE = HEAD_DIM ** -0.5 * math.log2(math.e)


def _head_norm_rope(t, gain, bd, cos, sin):
    ms = jnp.dot((t * t).astype(BF16), bd, preferred_element_type=F32)
    tn = t * lax.rsqrt(ms + EPS) * gain
    w = t.shape[-1]
    lane = lax.broadcasted_iota(jnp.int32, tn.shape, 1)
    nxt = pltpu.roll(tn, w - 1, 1)
    prv = pltpu.roll(tn, 1, 1)
    swapped = jnp.where((lane & 1) == 0, nxt, prv)
    return tn * cos + swapped * sin


def _inproj_kernel(x_ref, gpre_ref, w_ref, qg_ref, kg_ref, bdq_ref, bdk_ref,
                   cos_ref, sin_ref, hy_ref, q_ref, k_ref, v_ref):
    nhy = hy_ref.shape[-1]
    reps = ATT_Q // ATT_KV
    for rows in _row_halves(x_ref):
        u = _rms(x_ref[rows, :], gpre_ref[...]).astype(BF16)
        proj = jnp.dot(u, w_ref[...], preferred_element_type=F32)
        hy_ref[rows, :] = proj[:, :nhy]
        q = proj[:, nhy:nhy + ATT_Q]
        k = proj[:, nhy + ATT_Q:nhy + ATT_Q + ATT_KV]
        v = proj[:, nhy + ATT_Q + ATT_KV:]
        cos_k = cos_ref[rows, :]
        sin_k = sin_ref[rows, :]
        cos_q = jnp.concatenate([cos_k] * reps, axis=1)
        sin_q = jnp.concatenate([sin_k] * reps, axis=1)
        qr = _head_norm_rope(q, qg_ref[...], bdq_ref[...], cos_q, sin_q)
        kr = _head_norm_rope(k, kg_ref[...], bdk_ref[...], cos_k, sin_k)
        q_ref[rows, :] = (qr * Q_SCALE).astype(BF16)
        k_ref[rows, :] = kr.astype(BF16)
        v_ref[rows, :] = v.astype(BF16)


def _inproj(x2d, gpre, w, qg, kg, bdq, bdk, cos_t, sin_t, seq, tm):
    n, d = x2d.shape
    ncols = w.shape[1]
    nhy = ncols - ATT_Q - 2 * ATT_KV
    tiles_per_seq = seq // tm
    return pl.pallas_call(
        _inproj_kernel,
        grid=(n // tm,),
        in_specs=[
            pl.BlockSpec((tm, d), lambda i: (i, 0)),
            _const_spec((1, d)),
            _const_spec((d, ncols)),
            _const_spec((1, ATT_Q)), _const_spec((1, ATT_KV)),
            _const_spec((ATT_Q, ATT_Q)), _const_spec((ATT_KV, ATT_KV)),
            pl.BlockSpec((tm, ATT_KV), lambda i: (i % tiles_per_seq, 0)),
            pl.BlockSpec((tm, ATT_KV), lambda i: (i % tiles_per_seq, 0)),
        ],
        out_specs=[
            pl.BlockSpec((tm, nhy), lambda i: (i, 0)),
            pl.BlockSpec((tm, ATT_Q), lambda i: (i, 0)),
            pl.BlockSpec((tm, ATT_KV), lambda i: (i, 0)),
            pl.BlockSpec((tm, ATT_KV), lambda i: (i, 0)),
        ],
        out_shape=[
            jax.ShapeDtypeStruct((n, nhy), F32),
            jax.ShapeDtypeStruct((n, ATT_Q), BF16),
            jax.ShapeDtypeStruct((n, ATT_KV), BF16),
            jax.ShapeDtypeStruct((n, ATT_KV), BF16),
        ],
        compiler_params=_cparams(("parallel",)),
        name="inproj",
    )(x2d, gpre, w, qg, kg, bdq, bdk, cos_t, sin_t)


SHIFT_BOUND_MAX = 60.0
BOUND_SLACK = 1.02


BF16_SUBLANES = 2 * SUBLANES
V_ROWS = HEAD_DIM + BF16_SUBLANES


def _attn_shift_kernel(shift_ref, q_ref, k_ref, v_ref, o_ref, vt_ref):
    gw = GROUP * HEAD_DIM
    tq = q_ref.shape[0]
    seq = k_ref.shape[0]

    @pl.when(pl.program_id(1) == 0)
    def _():
        vt = v_ref[...].astype(F32).T
        row = lax.broadcasted_iota(jnp.int32, (V_ROWS - HEAD_DIM, seq), 0)
        ones = jnp.where(row == 0, 1.0, 0.0)
        for g in range(N_KV_HEADS):
            vg = vt[g * HEAD_DIM:(g + 1) * HEAD_DIM]
            vt_ref[g] = jnp.concatenate([vg, ones], axis=0).astype(BF16)

    kk = k_ref[...]
    shift = shift_ref[0:1, 0:1]
    lane = lax.broadcasted_iota(jnp.int32, (tq, ATT_KV), 1)
    for g in range(N_KV_HEADS):
        qg = q_ref[:, g * gw:(g + 1) * gw].astype(F32)
        lo = g * HEAD_DIM
        outs = []
        for h in range(GROUP):
            sh = (lo - h * HEAD_DIM) % gw
            qh = (qg if sh == 0 else pltpu.roll(qg, sh, 1))[:, :ATT_KV]
            qsel = jnp.where((lane >= lo) & (lane < lo + HEAD_DIM), qh, 0.0).astype(BF16)
            st = lax.dot_general(kk, qsel, (((1,), (1,)), ((), ())),
                                 preferred_element_type=F32)
            p = jnp.exp2(st - shift).astype(BF16)
            ot = jnp.dot(vt_ref[g], p, preferred_element_type=F32)
            outs.append(ot[:HEAD_DIM] * (1.0 / ot[HEAD_DIM:HEAD_DIM + 1]))
        og = jnp.concatenate(outs, axis=0)
        o_ref[:, g * gw:(g + 1) * gw] = og.T.astype(BF16)


def _attn_max_kernel(q_ref, k_ref, v_ref, o_ref, k0_ref, v0_ref):
    gw = GROUP * HEAD_DIM
    tq = q_ref.shape[0]

    @pl.when(pl.program_id(1) == 0)
    def _():
        lane = lax.broadcasted_iota(jnp.int32, (k_ref.shape[0], gw), 1)
        reps = gw // ATT_KV
        kk = jnp.concatenate([k_ref[...].astype(F32)] * reps, axis=1)
        vv = jnp.concatenate([v_ref[...].astype(F32)] * reps, axis=1)
        for g in range(N_KV_HEADS):
            ks = kk if g == 0 else pltpu.roll(kk, gw - g * HEAD_DIM, 1)
            vs = vv if g == 0 else pltpu.roll(vv, gw - g * HEAD_DIM, 1)
            k0_ref[g] = jnp.where(lane < HEAD_DIM, ks, 0.0).astype(BF16)
            v0_ref[g] = jnp.where(lane < HEAD_DIM, vs,
                                  jnp.where(lane == HEAD_DIM, 1.0, 0.0)).astype(BF16)

    lane_o = lax.broadcasted_iota(jnp.int32, (tq, gw), 1)
    for g in range(N_KV_HEADS):
        qg = q_ref[:, g * gw:(g + 1) * gw].astype(F32)
        acc = jnp.zeros((tq, gw), F32)
        for h in range(GROUP):
            qh = qg if h == 0 else pltpu.roll(qg, gw - h * HEAD_DIM, 1)
            s = lax.dot_general(qh.astype(BF16), k0_ref[g], (((1,), (1,)), ((), ())),
                                preferred_element_type=F32)
            m = jnp.max(s, axis=-1, keepdims=True)
            p = jnp.exp2(s - m)
            o = jnp.dot(p.astype(BF16), v0_ref[g], preferred_element_type=F32)
            l = o[:, HEAD_DIM:HEAD_DIM + 1]
            o = jnp.where(lane_o < HEAD_DIM, o * (1.0 / l), 0.0)
            acc = acc + (o if h == 0 else pltpu.roll(o, h * HEAD_DIM, 1))
        o_ref[:, g * gw:(g + 1) * gw] = acc.astype(BF16)


def _attention_call(shift, q, k, v, *, batch, seq, tq, use_shift):
    n = q.shape[0]
    gw = GROUP * HEAD_DIM
    qt = seq // tq
    qkv_specs = [
        pl.BlockSpec((tq, ATT_Q), lambda b, i: (b * qt + i, 0)),
        pl.BlockSpec((seq, ATT_KV), lambda b, i: (b, 0)),
        pl.BlockSpec((seq, ATT_KV), lambda b, i: (b, 0)),
    ]
    common = dict(
        grid=(batch, qt),
        out_specs=pl.BlockSpec((tq, ATT_Q), lambda b, i: (b * qt + i, 0)),
        out_shape=jax.ShapeDtypeStruct((n, ATT_Q), BF16),
        compiler_params=_cparams(("arbitrary", "arbitrary")),
    )
    if use_shift:
        return pl.pallas_call(
            _attn_shift_kernel,
            in_specs=[_const_spec(shift.shape)] + qkv_specs,
            scratch_shapes=[pltpu.VMEM((N_KV_HEADS, V_ROWS, seq), BF16)],
            name="attention_shift", **common,
        )(shift, q, k, v)
    return pl.pallas_call(
        _attn_max_kernel,
        in_specs=qkv_specs,
        scratch_shapes=[pltpu.VMEM((N_KV_HEADS, seq, gw), BF16),
                        pltpu.VMEM((N_KV_HEADS, seq, gw), BF16)],
        name="attention_max", **common,
    )(q, k, v)


def _attention(q, k, v, q_gain, k_gain, batch, seq, tq):
    bound = ((HEAD_DIM * Q_SCALE * BOUND_SLACK) * jnp.max(jnp.abs(q_gain))
             * jnp.max(jnp.abs(k_gain))).astype(F32)
    shift = jnp.full((SUBLANES, LANES), bound, F32)
    call = functools.partial(_attention_call, batch=batch, seq=seq, tq=tq)
    return lax.cond(bound < SHIFT_BOUND_MAX,
                    functools.partial(call, use_shift=True),
                    functools.partial(call, use_shift=False),
                    shift, q, k, v)


def _filt_mlp_kernel(z_ref, w1_ref, b1_ref, f1_ref, w2_ref, b2_ref, f2_ref,
                     w3f_ref, w3b_ref, df_ref, db_ref, hf_ref, hb_ref, h_ref):
    hp = lax.Precision.HIGHEST

    @pl.when(pl.program_id(0) == 0)
    def _():
        h1 = jnp.sin(f1_ref[...] * (jnp.dot(z_ref[...], w1_ref[...], precision=hp,
                                            preferred_element_type=F32) + b1_ref[...]))
        h_ref[...] = jnp.sin(f2_ref[...] * (jnp.dot(h1, w2_ref[...], precision=hp,
                                                    preferred_element_type=F32)
                                            + b2_ref[...]))

    h = h_ref[...]
    t = z_ref[:, 0:1]
    hf = jnp.dot(h, w3f_ref[...], precision=hp, preferred_element_type=F32)
    hb = jnp.dot(h, w3b_ref[...], precision=hp, preferred_element_type=F32)
    hf = hf * jnp.exp(-t * jnp.abs(df_ref[...]))
    hb = hb * jnp.exp(-t * jnp.abs(db_ref[...]))
    row = lax.broadcasted_iota(jnp.int32, hf.shape, 0)
    both = jnp.abs(hf) + jnp.abs(hb)
    merged = jnp.abs(hf + hb)
    s = jnp.sum(jnp.where(row == 0, merged, both), axis=0, keepdims=True)
    inv = 1.0 / s
    hf_ref[...] = hf * inv
    hb_ref[...] = hb * inv


def _filt_mlp(zf, w1, b1, f1, w2, b2, f2, w3f, w3b, df, db, tc):
    seq, ze = zf.shape
    ncol = w3f.shape[1]
    hid = w2.shape[0]
    col = lambda j: (0, j)
    return pl.pallas_call(
        _filt_mlp_kernel,
        grid=(ncol // tc,),
        in_specs=[
            _const_spec((seq, ze)), _const_spec((ze, hid)), _const_spec((1, hid)),
            _const_spec((1, hid)), _const_spec((hid, hid)), _const_spec((1, hid)),
            _const_spec((1, hid)),
            pl.BlockSpec((hid, tc), col), pl.BlockSpec((hid, tc), col),
            pl.BlockSpec((1, tc), col), pl.BlockSpec((1, tc), col),
        ],
        out_specs=[pl.BlockSpec((seq, tc), col), pl.BlockSpec((seq, tc), col)],
        out_shape=[jax.ShapeDtypeStruct((seq, ncol), F32),
                   jax.ShapeDtypeStruct((seq, ncol), F32)],
        scratch_shapes=[pltpu.VMEM((seq, hid), F32)],
        compiler_params=_cparams(("arbitrary",)),
        name="filt_mlp",
    )(zf, w1, b1, f1, w2, b2, f2, w3f, w3b, df, db)


def _phases(ref, rows):
    return [ref[pl.ds(q, rows, stride=NP), :] for q in range(NP)]


def _split_bf16(x):
    hi = x.astype(BF16)
    lo = (x - hi.astype(F32)).astype(BF16)
    return hi, lo


def _cmul_tw(xr, xi, c, s, conj):
    if conj:
        return xr * c - xi * s, xi * c + xr * s
    return xr * c + xi * s, xi * c - xr * s


def _fft_list(xs, sign):
    n = len(xs)
    if n == 1:
        return xs
    even = _fft_list(xs[0::2], sign)
    odd = _fft_list(xs[1::2], sign)
    half = n // 2
    out = [None] * n
    for k in range(half):
        er, ei = even[k]
        o_r, o_i = odd[k]
        if k == 0:
            tr, ti = o_r, o_i
            out[k] = (er + tr, ei + ti)
            out[k + half] = (er - tr, ei - ti)
        elif 4 * k == n:
            if sign > 0:
                out[k] = (er - o_i, ei + o_r)
                out[k + half] = (er + o_i, ei - o_r)
            else:
                out[k] = (er + o_i, ei - o_r)
                out[k + half] = (er - o_i, ei + o_r)
        else:
            c = math.cos(2.0 * math.pi * k / n)
            s = sign * math.sin(2.0 * math.pi * k / n)
            tr = o_r * c - o_i * s
            ti = o_r * s + o_i * c
            out[k] = (er + tr, ei + ti)
            out[k + half] = (er - tr, ei - ti)
    return out


def _filt_dft_kernel(hf_ref, hb_ref, fcs_ref, cos_ref, sin_ref, twc_ref, tws_ref,
                     hr_ref, hi_ref):
    rows = hf_ref.shape[0] // NP
    kp = hr_ref.shape[1]
    kf = fcs_ref.shape[0] // 2
    xs = _phases(hf_ref, rows) + _phases(hb_ref, rows)
    xcat = jnp.concatenate(xs, axis=1)
    hi, lo = _split_bf16(xcat)
    fcs = fcs_ref[...]
    a = (jnp.dot(fcs, hi, preferred_element_type=F32)
         + jnp.dot(fcs, lo, preferred_element_type=F32))
    c = cos_ref[...]
    s = sin_ref[...]

    def part(dirn, ph):
        j = (dirn * NP + ph) * LANES
        return a[:kp, j:j + LANES], a[kf:kf + kp, j:j + LANES]

    fr0, fi0 = part(0, 0)
    br0, bi0 = part(1, 0)
    gs = [(fr0 + br0, fi0 - bi0)]
    for d in range(1, NP):
        fr, fi = part(0, d)
        br, bi = part(1, NP - d)
        gr = fr + c * br + s * bi
        gi = fi + s * br - c * bi
        gs.append(_cmul_tw(gr, gi, twc_ref[d], tws_ref[d], False))
    hs = _fft_list(gs, -1)
    for k2 in range(NP):
        hr_ref[k2] = hs[k2][0] * (1.0 / NP)
        hi_ref[k2] = hs[k2][1] * (1.0 / NP)


def _filt_dft(hf, hb, fcs, cos_k, sin_k, twc, tws):
    seq, ncol = hf.shape
    kp = cos_k.shape[0]
    col = lambda j: (0, j)
    out_spec = pl.BlockSpec((NP, kp, LANES), lambda j: (0, 0, j))
    return pl.pallas_call(
        _filt_dft_kernel,
        grid=(ncol // LANES,),
        in_specs=[pl.BlockSpec((seq, LANES), col), pl.BlockSpec((seq, LANES), col),
                  _const_spec(fcs.shape), _const_spec(cos_k.shape),
                  _const_spec(sin_k.shape), _const_spec(twc.shape),
                  _const_spec(tws.shape)],
        out_specs=[out_spec, out_spec],
        out_shape=[jax.ShapeDtypeStruct((NP, kp, ncol), F32),
                   jax.ShapeDtypeStruct((NP, kp, ncol), F32)],
        compiler_params=_cparams(("parallel",)),
        name="filt_dft",
    )(hf, hb, fcs, cos_k, sin_k, twc, tws)


def _short_conv_phases(raw, w_ref, b_ref):
    rows = raw[0].shape[0]
    row = lax.broadcasted_iota(jnp.int32, raw[0].shape, 0)
    before = jnp.where(row == 0, 0.0, pltpu.roll(raw[NP - 1], 1, 0))
    after = jnp.where(row == rows - 1, 0.0, pltpu.roll(raw[0], rows - 1, 0))
    w0 = w_ref[0:1, :]
    w1 = w_ref[1:2, :]
    w2 = w_ref[2:3, :]
    b = b_ref[...]
    out = []
    for q in range(NP):
        prev = raw[q - 1] if q > 0 else before
        nxt = raw[q + 1] if q < NP - 1 else after
        out.append(prev * w0 + raw[q] * w1 + nxt * w2 + b)
    return out


def _longconv_kernel(x_ref, g_ref, wx_ref, bx_ref, wg_ref, bg_ref, bias_ref,
                     hr_ref, hi_ref, twc_ref, tws_ref, fcs_ref, ginv_ref, o_ref,
                     a_ref, b_ref, *, conv_x):
    rows = x_ref.shape[0] // NP
    kp = hr_ref.shape[1]
    kf = fcs_ref.shape[0] // 2
    xs = _phases(x_ref, rows)
    if conv_x:
        xs = _short_conv_phases(xs, wx_ref, bx_ref)
    gs = _short_conv_phases(_phases(g_ref, rows), wg_ref, bg_ref)

    xcat = jnp.concatenate([x.astype(BF16) for x in xs], axis=1)
    a_ref[...] = jnp.dot(fcs_ref[...], xcat, preferred_element_type=F32)

    def body(i):
        r0 = i * SUBLANES
        rs = pl.ds(r0, SUBLANES)
        zs = []
        for q in range(NP):
            ar = a_ref[rs, q * LANES:(q + 1) * LANES]
            ai = a_ref[pl.ds(kf + r0, SUBLANES), q * LANES:(q + 1) * LANES]
            zs.append((ar, ai) if q == 0 else
                      _cmul_tw(ar, ai, twc_ref[q, rs, :], tws_ref[q, rs, :], False))
        zs = _fft_list(zs, -1)
        ys = []
        for k2 in range(NP):
            zr, zi = zs[k2]
            hr = hr_ref[k2, rs, :]
            hi = hi_ref[k2, rs, :]
            ys.append((hr * zr - hi * zi, hr * zi + hi * zr))
        ys = _fft_list(ys, 1)
        for p in range(NP):
            br, bi = ys[p]
            if p > 0:
                br, bi = _cmul_tw(br, bi, twc_ref[p, rs, :], tws_ref[p, rs, :], True)
            b_ref[rs, p * LANES:(p + 1) * LANES] = br
            b_ref[pl.ds(kf + r0, SUBLANES), p * LANES:(p + 1) * LANES] = bi

    for i in range(kp // SUBLANES):
        body(i)
    if kf > kp:
        zeros = jnp.zeros((kf - kp, NP * LANES), F32)
        b_ref[kp:kf, :] = zeros
        b_ref[kf + kp:, :] = zeros

    y = jnp.dot(ginv_ref[...], b_ref[...].astype(BF16), preferred_element_type=F32)
    bias = bias_ref[...]
    for p in range(NP):
        yp = y[:, p * LANES:(p + 1) * LANES]
        o_ref[pl.ds(p, rows, stride=NP), :] = gs[p] * (yp + xs[p] * bias)


def _longconv(x3, x_blk0, hy3, g_blk0, sw, sb, wx_blk0, bias, h_blk0, hr, hi,
              twc, tws, fcs, ginv, conv_x):
    batch, seq, _ = hy3.shape
    nch = bias.shape[1] // LANES
    kp = hr.shape[1]
    kernel = functools.partial(_longconv_kernel, conv_x=conv_x)
    return pl.pallas_call(
        kernel,
        grid=(nch, batch),
        in_specs=[
            pl.BlockSpec((None, seq, LANES), lambda c, b: (b, 0, x_blk0 + c)),
            pl.BlockSpec((None, seq, LANES), lambda c, b: (b, 0, g_blk0 + c)),
            pl.BlockSpec((3, LANES), lambda c, b: (0, wx_blk0 + c)),
            pl.BlockSpec((1, LANES), lambda c, b: (0, wx_blk0 + c)),
            pl.BlockSpec((3, LANES), lambda c, b: (0, g_blk0 + c)),
            pl.BlockSpec((1, LANES), lambda c, b: (0, g_blk0 + c)),
            pl.BlockSpec((1, LANES), lambda c, b: (0, c)),
            pl.BlockSpec((NP, kp, LANES), lambda c, b: (0, 0, h_blk0 + c),
                         pipeline_mode=pl.Buffered(1)),
            pl.BlockSpec((NP, kp, LANES), lambda c, b: (0, 0, h_blk0 + c),
                         pipeline_mode=pl.Buffered(1)),
            _const_spec(twc.shape), _const_spec(tws.shape),
            _const_spec(fcs.shape), _const_spec(ginv.shape),
        ],
        out_specs=pl.BlockSpec((None, seq, LANES), lambda c, b: (b, 0, c)),
        out_shape=jax.ShapeDtypeStruct((batch, seq, nch * LANES), F32),
        scratch_shapes=[pltpu.VMEM((fcs.shape[0], NP * LANES), F32),
                        pltpu.VMEM((fcs.shape[0], NP * LANES), F32)],
        compiler_params=_cparams(("arbitrary", "arbitrary")),
        name="longconv_x" if conv_x else "longconv",
    )(x3, hy3, sw, sb, sw, sb, bias, hr, hi, twc, tws, fcs, ginv)


def _merge_kernel(x_ref, ya_ref, yb_ref, gpre_ref, gpost_ref, wgate_ref,
                  wa_ref, wb_ref, wo_ref, o_ref):
    d = x_ref.shape[-1]
    for rows in _row_halves(x_ref):
        x = x_ref[rows, :]
        u = _rms(x, gpre_ref[...]).astype(BF16)
        gates = jnp.dot(u, wgate_ref[...], preferred_element_type=F32)
        ma = jnp.dot(ya_ref[rows, :].astype(BF16), wa_ref[...], preferred_element_type=F32)
        mb = jnp.dot(yb_ref[rows, :], wb_ref[...], preferred_element_type=F32)
        m = _sigmoid(gates[:, :d]) * ma + _sigmoid(gates[:, d:]) * mb
        y = jnp.dot(m.astype(BF16), wo_ref[...], preferred_element_type=F32)
        o_ref[rows, :] = x + _rms(y, gpost_ref[...])


def _merge(x2d, ya, yb, gpre, gpost, wgate, wa, wb, wo, tm):
    n, d = x2d.shape
    row = lambda i: (i, 0)
    return pl.pallas_call(
        _merge_kernel,
        grid=(n // tm,),
        in_specs=[
            pl.BlockSpec((tm, d), row),
            pl.BlockSpec((tm, ya.shape[1]), row),
            pl.BlockSpec((tm, yb.shape[1]), row),
            _const_spec((1, d)), _const_spec((1, d)),
            _const_spec(wgate.shape), _const_spec(wa.shape),
            _const_spec(wb.shape), _const_spec(wo.shape),
        ],
        out_specs=pl.BlockSpec((tm, d), row),
        out_shape=jax.ShapeDtypeStruct((n, d), F32),
        compiler_params=_cparams(("parallel",)),
        name="merge",
    )(x2d, ya, yb, gpre, gpost, wgate, wa, wb, wo)


def _ffn_ple_kernel(x_ref, gpre_ref, gpost_ref, wg_ref, wu_ref, wd_ref,
                    p_ref, ppre_ref, ppost_ref, wpg_ref, wpp_ref, o_ref):
    x1 = _ffn_block(x_ref[...], gpre_ref, gpost_ref, wg_ref, wu_ref, wd_ref)
    u = _rms(x1, ppre_ref[...]).astype(BF16)
    g = _sigmoid(jnp.dot(u, wpg_ref[...], preferred_element_type=F32))
    e = jnp.dot(p_ref[...].astype(BF16), wpp_ref[...], preferred_element_type=F32)
    o_ref[...] = x1 + _rms(g * e, ppost_ref[...])


def _ffn_ple(x2d, gpre, gpost, wg, wu, wd, p2d, ppre, ppost, wpg, wpp, tm):
    n, d = x2d.shape
    row = lambda i: (i, 0)
    return pl.pallas_call(
        _ffn_ple_kernel,
        grid=(n // tm,),
        in_specs=[
            pl.BlockSpec((tm, d), row),
            _const_spec((1, d)), _const_spec((1, d)),
            _const_spec(wg.shape), _const_spec(wu.shape), _const_spec(wd.shape),
            pl.BlockSpec((tm, p2d.shape[1]), row),
            _const_spec((1, d)), _const_spec((1, d)),
            _const_spec(wpg.shape), _const_spec(wpp.shape),
        ],
        out_specs=pl.BlockSpec((tm, d), row),
        out_shape=jax.ShapeDtypeStruct((n, d), F32),
        compiler_params=_cparams(("parallel",)),
        name="ffn_ple",
    )(x2d, gpre, gpost, wg, wu, wd, p2d, ppre, ppost, wpg, wpp)


def _rope_tables(seq):
    rows = seq // GRID_W
    row = np.repeat(np.arange(rows, dtype=np.float32), GRID_W)
    col = np.tile(np.arange(GRID_W, dtype=np.float32), rows)
    inv = (ROPE_THETA ** (-np.arange(0, AXIS_DIM, 2, dtype=np.float32) / AXIS_DIM)
           ).astype(np.float32)
    ang = np.concatenate([row[:, None] * inv, col[:, None] * inv], axis=-1)
    cos = np.repeat(np.cos(ang), 2, axis=1)
    sin = np.repeat(np.sin(ang), 2, axis=1)
    sign = np.tile(np.array([-1.0, 1.0], np.float32), HEAD_DIM // 2)
    cos_t = np.tile(cos, (1, N_KV_HEADS)).astype(np.float32)
    sin_t = np.tile(sin * sign, (1, N_KV_HEADS)).astype(np.float32)
    return jnp.asarray(cos_t), jnp.asarray(sin_t)


def _block_diag_mean(width):
    idx = np.arange(width) // HEAD_DIM
    return jnp.asarray((idx[:, None] == idx[None, :]).astype(np.float32) / HEAD_DIM,
                       dtype=BF16)


def _filter_features(seq, pad_to):
    t = np.linspace(0.0, 1.0, seq, dtype=np.float32)[:, None]
    w = ((2.0 * math.pi / seq) * np.arange(seq, dtype=np.float32)).astype(np.float32)
    bands = np.linspace(1e-4, FILT_BANDS - 1, FILT_BANDS, dtype=np.float32)
    ang = w[:, None] * bands[None, :]
    z = np.concatenate([t, np.cos(ang), -np.sin(ang)], axis=-1).astype(np.float32)
    return jnp.asarray(np.pad(z, ((0, 0), (0, pad_to - z.shape[1]))))


def _dft_tables(seq):
    m1 = seq // NP
    nf = 2 * m1
    nk = nf // 2 + 1
    kp = -(-nk // SUBLANES) * SUBLANES
    half_tile = LANES // 2
    kf = -(-nk // half_tile) * half_tile
    k = np.arange(nk, dtype=np.float64)[:, None]
    m = np.arange(m1, dtype=np.float64)[None, :]
    th = 2.0 * np.pi * k * m / nf
    fcs = np.zeros((2 * kf, m1), np.float64)
    fcs[:nk] = np.cos(th)
    fcs[kf:kf + nk] = -np.sin(th)
    ck = np.full((nk,), 2.0 / nf)
    ck[0] = 1.0 / nf
    ck[-1] = 1.0 / nf
    ginv = np.zeros((m1, 2 * kf), np.float64)
    ginv[:, :nk] = (np.cos(th) * ck[:, None]).T
    ginv[:, kf:kf + nk] = (-np.sin(th) * ck[:, None]).T
    th1 = 2.0 * np.pi * np.arange(kp, dtype=np.float64) / nf
    th1[nk:] = 0.0
    cos_k = np.repeat(np.cos(th1)[:, None], LANES, axis=1)
    sin_k = np.repeat(np.sin(th1)[:, None], LANES, axis=1)
    thq = th1[None, :] * np.arange(NP, dtype=np.float64)[:, None] / NP
    twc = np.repeat(np.cos(thq)[:, :, None], LANES, axis=2)
    tws = np.repeat(np.sin(thq)[:, :, None], LANES, axis=2)
    return tuple(jnp.asarray(t, dtype=F32) for t in (fcs, ginv, cos_k, sin_k, twc, tws))


def _pick_tile(n, want):
    t = min(n, want)
    while n % t:
        t //= 2
    return t


def kernel(x, p, ffn1_norm_pre, ffn1_norm_post, ffn1_w_gate, ffn1_w_up, ffn1_w_down,
           mix_norm_pre, mix_norm_post, w_in, hy_short_w, hy_short_b,
           filt_w1, filt_b1, filt_freq1, filt_w2, filt_b2, filt_freq2, filt_w3,
           filt_deltas, hy_bias, q_norm, k_norm, w_hy_out, w_att_out, w_out,
           ffn2_norm_pre, ffn2_norm_post, ffn2_w_gate, ffn2_w_up, ffn2_w_down,
           ple_norm_pre, ple_norm_post, w_ple_gate, w_ple_proj):
    batch, seq, d = x.shape
    depth = p.shape[0]
    n = batch * seq
    nhy = 3 * HY_WIDTH
    nmix = nhy + ATT_Q + 2 * ATT_KV
    hy_blocks = HY_WIDTH // LANES

    tm = _pick_tile(seq, 512)
    tq = _pick_tile(seq, 512)
    cos_t, sin_t = _rope_tables(seq)
    bdq = _block_diag_mean(ATT_Q)
    bdk = _block_diag_mean(ATT_KV)
    ze = LANES
    zf = _filter_features(seq, ze)
    row = lambda a: a.reshape(1, -1).astype(F32)
    bf = lambda a: a.astype(BF16)
    fcs, ginv, cos_k, sin_k, twc, tws = _dft_tables(seq)
    fcs, ginv = bf(fcs), bf(ginv)

    x2d = x.reshape(n, d)
    for i in range(depth):
        x2d = _ffn(x2d, row(ffn1_norm_pre[i]), row(ffn1_norm_post[i]),
                   bf(ffn1_w_gate[i]), bf(ffn1_w_up[i]), bf(ffn1_w_down[i]), tm)

        hy, q, k, v = _inproj(
            x2d, row(mix_norm_pre[i]), bf(w_in[i][:, :nmix]),
            row(jnp.tile(q_norm[i], N_Q_HEADS)), row(jnp.tile(k_norm[i], N_KV_HEADS)),
            bdq, bdk, cos_t, sin_t, seq, tm)

        w3 = filt_w3[i].reshape(FILT_HID, HY_ORDER, 2, HY_WIDTH)
        dl = filt_deltas[i]
        w1p = jnp.pad(filt_w1[i], ((0, ze - FILT_EMB), (0, 0)))
        hf, hb = _filt_mlp(
            zf, w1p, row(filt_b1[i]), row(filt_freq1[i]), filt_w2[i],
            row(filt_b2[i]), row(filt_freq2[i]),
            w3[:, :, 0].reshape(FILT_HID, -1), w3[:, :, 1].reshape(FILT_HID, -1),
            row(dl[:, 0]), row(dl[:, 1]), LANES)
        hr, hi = _filt_dft(hf, hb, fcs, cos_k, sin_k, twc, tws)

        hy3 = hy.reshape(batch, seq, nhy)
        sw = hy_short_w[i]
        sb = row(hy_short_b[i])
        z1 = _longconv(hy3, 0, hy3, hy_blocks, sw, sb, 0, row(hy_bias[i][0]),
                       0, hr, hi, twc, tws, fcs, ginv, True)
        ya = _longconv(z1, 0, hy3, 2 * hy_blocks, sw, sb, 0, row(hy_bias[i][1]),
                       hy_blocks, hr, hi, twc, tws, fcs, ginv, False)

        yb = _attention(q, k, v, q_norm[i], k_norm[i], batch, seq, tq)

        x2d = _merge(x2d, ya.reshape(n, HY_WIDTH), yb, row(mix_norm_pre[i]),
                     row(mix_norm_post[i]), bf(w_in[i][:, nmix:]), bf(w_hy_out[i]),
                     bf(w_att_out[i]), bf(w_out[i]), tm)

        x2d = _ffn_ple(x2d, row(ffn2_norm_pre[i]), row(ffn2_norm_post[i]),
                       bf(ffn2_w_gate[i]), bf(ffn2_w_up[i]), bf(ffn2_w_down[i]),
                       p[i].reshape(n, -1), row(ple_norm_pre[i]), row(ple_norm_post[i]),
                       bf(w_ple_gate[i]), bf(w_ple_proj[i]), tm)
    return x2d.reshape(batch, seq, d)
```

```python
import functools
import math

import numpy as np
import jax
import jax.numpy as jnp
from jax import lax
from jax.experimental import pallas as pl
from jax.experimental.pallas import tpu as pltpu

F32 = jnp.float32
BF16 = jnp.bfloat16

GRID_W = 64
HY_WIDTH = 512
HY_ORDER = 2
FILT_EMB = 33
FILT_BANDS = (FILT_EMB - 1) // 2
FILT_HID = 64
N_Q_HEADS = 8
N_KV_HEADS = 2
GROUP = N_Q_HEADS // N_KV_HEADS
HEAD_DIM = 64
AXIS_DIM = HEAD_DIM // 2
ROPE_THETA = 10000.0
EPS = 1e-6
ATT_Q = N_Q_HEADS * HEAD_DIM
ATT_KV = N_KV_HEADS * HEAD_DIM

LANES = 128
SUBLANES = 8
VMEM_LIMIT_BYTES = 60 * 1024 * 1024

NP = 8


def _cparams(sem):
    return pltpu.CompilerParams(dimension_semantics=sem,
                                vmem_limit_bytes=VMEM_LIMIT_BYTES)


def _const_spec(shape):
    nd = len(shape)
    return pl.BlockSpec(shape, lambda *_: (0,) * nd, pipeline_mode=pl.Buffered(1))


def _rms(x, g):
    return x * lax.rsqrt(jnp.mean(x * x, axis=-1, keepdims=True) + EPS) * g


def _sigmoid(x):
    return 1.0 / (1.0 + jnp.exp(-x))


def _ffn_block(x, gpre_ref, gpost_ref, wg_ref, wu_ref, wd_ref):
    xn = _rms(x, gpre_ref[...]).astype(BF16)
    g = jnp.dot(xn, wg_ref[...], preferred_element_type=F32)
    u = jnp.dot(xn, wu_ref[...], preferred_element_type=F32)
    h = (g * _sigmoid(g) * u).astype(BF16)
    y = jnp.dot(h, wd_ref[...], preferred_element_type=F32)
    return x + 0.5 * _rms(y, gpost_ref[...])


def _row_halves(ref):
    half = ref.shape[0] // 2
    return (slice(0, half), slice(half, 2 * half))


def _ffn_kernel(x_ref, gpre_ref, gpost_ref, wg_ref, wu_ref, wd_ref, o_ref):
    for rows in _row_halves(x_ref):
        o_ref[rows, :] = _ffn_block(x_ref[rows, :], gpre_ref, gpost_ref,
                                    wg_ref, wu_ref, wd_ref)


def _ffn(x2d, gpre, gpost, wg, wu, wd, tm):
    n, d = x2d.shape
    ff = wg.shape[1]
    return pl.pallas_call(
        _ffn_kernel,
        grid=(n // tm,),
        in_specs=[
            pl.BlockSpec((tm, d), lambda i: (i, 0)),
            _const_spec((1, d)), _const_spec((1, d)),
            _const_spec((d, ff)), _const_spec((d, ff)), _const_spec((ff, d)),
        ],
        out_specs=pl.BlockSpec((tm, d), lambda i: (i, 0)),
        out_shape=jax.ShapeDtypeStruct((n, d), F32),
        compiler_params=_cparams(("parallel",)),
        name="ffn",
    )(x2d, gpre, gpost, wg, wu, wd)


Q_SCALE = HEAD_DIM ** -0.5 * math.log2(math.e)


def _head_norm_rope(t, gain, bd, cos, sin):
    ms = jnp.dot((t * t).astype(BF16), bd, preferred_element_type=F32)
    tn = t * lax.rsqrt(ms + EPS) * gain
    w = t.shape[-1]
    lane = lax.broadcasted_iota(jnp.int32, tn.shape, 1)
    nxt = pltpu.roll(tn, w - 1, 1)
    prv = pltpu.roll(tn, 1, 1)
    swapped = jnp.where((lane & 1) == 0, nxt, prv)
    return tn * cos + swapped * sin


def _inproj_kernel(x_ref, gpre_ref, w_ref, qg_ref, kg_ref, bdq_ref, bdk_ref,
                   cos_ref, sin_ref, hy_ref, q_ref, k_ref, v_ref):
    nhy = hy_ref.shape[-1]
    reps = ATT_Q // ATT_KV
    for rows in _row_halves(x_ref):
        u = _rms(x_ref[rows, :], gpre_ref[...]).astype(BF16)
        proj = jnp.dot(u, w_ref[...], preferred_element_type=F32)
        hy_ref[rows, :] = proj[:, :nhy]
        q = proj[:, nhy:nhy + ATT_Q]
        k = proj[:, nhy + ATT_Q:nhy + ATT_Q + ATT_KV]
        v = proj[:, nhy + ATT_Q + ATT_KV:]
        cos_k = cos_ref[rows, :]
        sin_k = sin_ref[rows, :]
        cos_q = jnp.concatenate([cos_k] * reps, axis=1)
        sin_q = jnp.concatenate([sin_k] * reps, axis=1)
        qr = _head_norm_rope(q, qg_ref[...], bdq_ref[...], cos_q, sin_q)
        kr = _head_norm_rope(k, kg_ref[...], bdk_ref[...], cos_k, sin_k)
        q_ref[rows, :] = (qr * Q_SCALE).astype(BF16)
        k_ref[rows, :] = kr.astype(BF16)
        v_ref[rows, :] = v.astype(BF16)


def _inproj(x2d, gpre, w, qg, kg, bdq, bdk, cos_t, sin_t, seq, tm):
    n, d = x2d.shape
    ncols = w.shape[1]
    nhy = ncols - ATT_Q - 2 * ATT_KV
    tiles_per_seq = seq // tm
    return pl.pallas_call(
        _inproj_kernel,
        grid=(n // tm,),
        in_specs=[
            pl.BlockSpec((tm, d), lambda i: (i, 0)),
            _const_spec((1, d)),
            _const_spec((d, ncols)),
            _const_spec((1, ATT_Q)), _const_spec((1, ATT_KV)),
            _const_spec((ATT_Q, ATT_Q)), _const_spec((ATT_KV, ATT_KV)),
            pl.BlockSpec((tm, ATT_KV), lambda i: (i % tiles_per_seq, 0)),
            pl.BlockSpec((tm, ATT_KV), lambda i: (i % tiles_per_seq, 0)),
        ],
        out_specs=[
            pl.BlockSpec((tm, nhy), lambda i: (i, 0)),
            pl.BlockSpec((tm, ATT_Q), lambda i: (i, 0)),
            pl.BlockSpec((tm, ATT_KV), lambda i: (i, 0)),
            pl.BlockSpec((tm, ATT_KV), lambda i: (i, 0)),
        ],
        out_shape=[
            jax.ShapeDtypeStruct((n, nhy), F32),
            jax.ShapeDtypeStruct((n, ATT_Q), BF16),
            jax.ShapeDtypeStruct((n, ATT_KV), BF16),
            jax.ShapeDtypeStruct((n, ATT_KV), BF16),
        ],
        compiler_params=_cparams(("parallel",)),
        name="inproj",
    )(x2d, gpre, w, qg, kg, bdq, bdk, cos_t, sin_t)


SHIFT_BOUND_MAX = 60.0
BOUND_SLACK = 1.02


BF16_SUBLANES = 2 * SUBLANES
V_ROWS = HEAD_DIM + BF16_SUBLANES


def _attn_shift_kernel(shift_ref, q_ref, k_ref, v_ref, o_ref, vt_ref):
    gw = GROUP * HEAD_DIM
    tq = q_ref.shape[0]
    seq = k_ref.shape[0]

    @pl.when(pl.program_id(1) == 0)
    def _():
        vt = v_ref[...].astype(F32).T
        row = lax.broadcasted_iota(jnp.int32, (V_ROWS - HEAD_DIM, seq), 0)
        ones = jnp.where(row == 0, 1.0, 0.0)
        for g in range(N_KV_HEADS):
            vg = vt[g * HEAD_DIM:(g + 1) * HEAD_DIM]
            vt_ref[g] = jnp.concatenate([vg, ones], axis=0).astype(BF16)

    kk = k_ref[...]
    shift = shift_ref[0:1, 0:1]
    lane = lax.broadcasted_iota(jnp.int32, (tq, ATT_KV), 1)
    for g in range(N_KV_HEADS):
        qg = q_ref[:, g * gw:(g + 1) * gw].astype(F32)
        lo = g * HEAD_DIM
        outs = []
        for h in range(GROUP):
            sh = (lo - h * HEAD_DIM) % gw
            qh = (qg if sh == 0 else pltpu.roll(qg, sh, 1))[:, :ATT_KV]
            qsel = jnp.where((lane >= lo) & (lane < lo + HEAD_DIM), qh, 0.0).astype(BF16)
            st = lax.dot_general(kk, qsel, (((1,), (1,)), ((), ())),
                                 preferred_element_type=F32)
            p = jnp.exp2(st - shift).astype(BF16)
            ot = jnp.dot(vt_ref[g], p, preferred_element_type=F32)
            outs.append(ot[:HEAD_DIM] * (1.0 / ot[HEAD_DIM:HEAD_DIM + 1]))
        og = jnp.concatenate(outs, axis=0)
        o_ref[:, g * gw:(g + 1) * gw] = og.T.astype(BF16)


def _attn_max_kernel(q_ref, k_ref, v_ref, o_ref, k0_ref, v0_ref):
    gw = GROUP * HEAD_DIM
    tq = q_ref.shape[0]

    @pl.when(pl.program_id(1) == 0)
    def _():
        lane = lax.broadcasted_iota(jnp.int32, (k_ref.shape[0], gw), 1)
        reps = gw // ATT_KV
        kk = jnp.concatenate([k_ref[...].astype(F32)] * reps, axis=1)
        vv = jnp.concatenate([v_ref[...].astype(F32)] * reps, axis=1)
        for g in range(N_KV_HEADS):
            ks = kk if g == 0 else pltpu.roll(kk, gw - g * HEAD_DIM, 1)
            vs = vv if g == 0 else pltpu.roll(vv, gw - g * HEAD_DIM, 1)
            k0_ref[g] = jnp.where(lane < HEAD_DIM, ks, 0.0).astype(BF16)
            v0_ref[g] = jnp.where(lane < HEAD_DIM, vs,
                                  jnp.where(lane == HEAD_DIM, 1.0, 0.0)).astype(BF16)

    lane_o = lax.broadcasted_iota(jnp.int32, (tq, gw), 1)
    for g in range(N_KV_HEADS):
        qg = q_ref[:, g * gw:(g + 1) * gw].astype(F32)
        acc = jnp.zeros((tq, gw), F32)
        for h in range(GROUP):
            qh = qg if h == 0 else pltpu.roll(qg, gw - h * HEAD_DIM, 1)
            s = lax.dot_general(qh.astype(BF16), k0_ref[g], (((1,), (1,)), ((), ())),
                                preferred_element_type=F32)
            m = jnp.max(s, axis=-1, keepdims=True)
            p = jnp.exp2(s - m)
            o = jnp.dot(p.astype(BF16), v0_ref[g], preferred_element_type=F32)
            l = o[:, HEAD_DIM:HEAD_DIM + 1]
            o = jnp.where(lane_o < HEAD_DIM, o * (1.0 / l), 0.0)
            acc = acc + (o if h == 0 else pltpu.roll(o, h * HEAD_DIM, 1))
        o_ref[:, g * gw:(g + 1) * gw] = acc.astype(BF16)


def _attention_call(shift, q, k, v, *, batch, seq, tq, use_shift):
    n = q.shape[0]
    gw = GROUP * HEAD_DIM
    qt = seq // tq
    qkv_specs = [
        pl.BlockSpec((tq, ATT_Q), lambda b, i: (b * qt + i, 0)),
        pl.BlockSpec((seq, ATT_KV), lambda b, i: (b, 0)),
        pl.BlockSpec((seq, ATT_KV), lambda b, i: (b, 0)),
    ]
    common = dict(
        grid=(batch, qt),
        out_specs=pl.BlockSpec((tq, ATT_Q), lambda b, i: (b * qt + i, 0)),
        out_shape=jax.ShapeDtypeStruct((n, ATT_Q), BF16),
        compiler_params=_cparams(("arbitrary", "arbitrary")),
    )
    if use_shift:
        return pl.pallas_call(
            _attn_shift_kernel,
            in_specs=[_const_spec(shift.shape)] + qkv_specs,
            scratch_shapes=[pltpu.VMEM((N_KV_HEADS, V_ROWS, seq), BF16)],
            name="attention_shift", **common,
        )(shift, q, k, v)
    return pl.pallas_call(
        _attn_max_kernel,
        in_specs=qkv_specs,
        scratch_shapes=[pltpu.VMEM((N_KV_HEADS, seq, gw), BF16),
                        pltpu.VMEM((N_KV_HEADS, seq, gw), BF16)],
        name="attention_max", **common,
    )(q, k, v)


def _attention(q, k, v, q_gain, k_gain, batch, seq, tq):
    bound = ((HEAD_DIM * Q_SCALE * BOUND_SLACK) * jnp.max(jnp.abs(q_gain))
             * jnp.max(jnp.abs(k_gain))).astype(F32)
    shift = jnp.full((SUBLANES, LANES), bound, F32)
    call = functools.partial(_attention_call, batch=batch, seq=seq, tq=tq)
    return lax.cond(bound < SHIFT_BOUND_MAX,
                    functools.partial(call, use_shift=True),
                    functools.partial(call, use_shift=False),
                    shift, q, k, v)


def _filt_mlp_kernel(z_ref, w1_ref, b1_ref, f1_ref, w2_ref, b2_ref, f2_ref,
                     w3f_ref, w3b_ref, df_ref, db_ref, hf_ref, hb_ref, h_ref):
    hp = lax.Precision.HIGHEST

    @pl.when(pl.program_id(0) == 0)
    def _():
        h1 = jnp.sin(f1_ref[...] * (jnp.dot(z_ref[...], w1_ref[...], precision=hp,
                                            preferred_element_type=F32) + b1_ref[...]))
        h_ref[...] = jnp.sin(f2_ref[...] * (jnp.dot(h1, w2_ref[...], precision=hp,
                                                    preferred_element_type=F32)
                                            + b2_ref[...]))

    h = h_ref[...]
    t = z_ref[:, 0:1]
    hf = jnp.dot(h, w3f_ref[...], precision=hp, preferred_element_type=F32)
    hb = jnp.dot(h, w3b_ref[...], precision=hp, preferred_element_type=F32)
    hf = hf * jnp.exp(-t * jnp.abs(df_ref[...]))
    hb = hb * jnp.exp(-t * jnp.abs(db_ref[...]))
    row = lax.broadcasted_iota(jnp.int32, hf.shape, 0)
    both = jnp.abs(hf) + jnp.abs(hb)
    merged = jnp.abs(hf + hb)
    s = jnp.sum(jnp.where(row == 0, merged, both), axis=0, keepdims=True)
    inv = 1.0 / s
    hf_ref[...] = hf * inv
    hb_ref[...] = hb * inv


def _filt_mlp(zf, w1, b1, f1, w2, b2, f2, w3f, w3b, df, db, tc):
    seq, ze = zf.shape
    ncol = w3f.shape[1]
    hid = w2.shape[0]
    col = lambda j: (0, j)
    return pl.pallas_call(
        _filt_mlp_kernel,
        grid=(ncol // tc,),
        in_specs=[
            _const_spec((seq, ze)), _const_spec((ze, hid)), _const_spec((1, hid)),
            _const_spec((1, hid)), _const_spec((hid, hid)), _const_spec((1, hid)),
            _const_spec((1, hid)),
            pl.BlockSpec((hid, tc), col), pl.BlockSpec((hid, tc), col),
            pl.BlockSpec((1, tc), col), pl.BlockSpec((1, tc), col),
        ],
        out_specs=[pl.BlockSpec((seq, tc), col), pl.BlockSpec((seq, tc), col)],
        out_shape=[jax.ShapeDtypeStruct((seq, ncol), F32),
                   jax.ShapeDtypeStruct((seq, ncol), F32)],
        scratch_shapes=[pltpu.VMEM((seq, hid), F32)],
        compiler_params=_cparams(("arbitrary",)),
        name="filt_mlp",
    )(zf, w1, b1, f1, w2, b2, f2, w3f, w3b, df, db)


def _phases(ref, rows):
    return [ref[pl.ds(q, rows, stride=NP), :] for q in range(NP)]


def _split_bf16(x):
    hi = x.astype(BF16)
    lo = (x - hi.astype(F32)).astype(BF16)
    return hi, lo


def _cmul_tw(xr, xi, c, s, conj):
    if conj:
        return xr * c - xi * s, xi * c + xr * s
    return xr * c + xi * s, xi * c - xr * s


def _fft_list(xs, sign):
    n = len(xs)
    if n == 1:
        return xs
    even = _fft_list(xs[0::2], sign)
    odd = _fft_list(xs[1::2], sign)
    half = n // 2
    out = [None] * n
    for k in range(half):
        er, ei = even[k]
        o_r, o_i = odd[k]
        if k == 0:
            tr, ti = o_r, o_i
            out[k] = (er + tr, ei + ti)
            out[k + half] = (er - tr, ei - ti)
        elif 4 * k == n:
            if sign > 0:
                out[k] = (er - o_i, ei + o_r)
                out[k + half] = (er + o_i, ei - o_r)
            else:
                out[k] = (er + o_i, ei - o_r)
                out[k + half] = (er - o_i, ei + o_r)
        else:
            c = math.cos(2.0 * math.pi * k / n)
            s = sign * math.sin(2.0 * math.pi * k / n)
            tr = o_r * c - o_i * s
            ti = o_r * s + o_i * c
            out[k] = (er + tr, ei + ti)
            out[k + half] = (er - tr, ei - ti)
    return out


def _filt_dft_kernel(hf_ref, hb_ref, fcs_ref, cos_ref, sin_ref, twc_ref, tws_ref,
                     hr_ref, hi_ref):
    rows = hf_ref.shape[0] // NP
    kp = hr_ref.shape[1]
    kf = fcs_ref.shape[0] // 2
    xs = _phases(hf_ref, rows) + _phases(hb_ref, rows)
    xcat = jnp.concatenate(xs, axis=1)
    hi, lo = _split_bf16(xcat)
    fcs = fcs_ref[...]
    a = (jnp.dot(fcs, hi, preferred_element_type=F32)
         + jnp.dot(fcs, lo, preferred_element_type=F32))
    c = cos_ref[...]
    s = sin_ref[...]

    def part(dirn, ph):
        j = (dirn * NP + ph) * LANES
        return a[:kp, j:j + LANES], a[kf:kf + kp, j:j + LANES]

    fr0, fi0 = part(0, 0)
    br0, bi0 = part(1, 0)
    gs = [(fr0 + br0, fi0 - bi0)]
    for d in range(1, NP):
        fr, fi = part(0, d)
        br, bi = part(1, NP - d)
        gr = fr + c * br + s * bi
        gi = fi + s * br - c * bi
        gs.append(_cmul_tw(gr, gi, twc_ref[d], tws_ref[d], False))
    hs = _fft_list(gs, -1)
    for k2 in range(NP):
        hr_ref[k2] = hs[k2][0] * (1.0 / NP)
        hi_ref[k2] = hs[k2][1] * (1.0 / NP)


def _filt_dft(hf, hb, fcs, cos_k, sin_k, twc, tws):
    seq, ncol = hf.shape
    kp = cos_k.shape[0]
    col = lambda j: (0, j)
    out_spec = pl.BlockSpec((NP, kp, LANES), lambda j: (0, 0, j))
    return pl.pallas_call(
        _filt_dft_kernel,
        grid=(ncol // LANES,),
        in_specs=[pl.BlockSpec((seq, LANES), col), pl.BlockSpec((seq, LANES), col),
                  _const_spec(fcs.shape), _const_spec(cos_k.shape),
                  _const_spec(sin_k.shape), _const_spec(twc.shape),
                  _const_spec(tws.shape)],
        out_specs=[out_spec, out_spec],
        out_shape=[jax.ShapeDtypeStruct((NP, kp, ncol), F32),
                   jax.ShapeDtypeStruct((NP, kp, ncol), F32)],
        compiler_params=_cparams(("parallel",)),
        name="filt_dft",
    )(hf, hb, fcs, cos_k, sin_k, twc, tws)


def _short_conv_phases(raw, w_ref, b_ref):
    rows = raw[0].shape[0]
    row = lax.broadcasted_iota(jnp.int32, raw[0].shape, 0)
    before = jnp.where(row == 0, 0.0, pltpu.roll(raw[NP - 1], 1, 0))
    after = jnp.where(row == rows - 1, 0.0, pltpu.roll(raw[0], rows - 1, 0))
    w0 = w_ref[0:1, :]
    w1 = w_ref[1:2, :]
    w2 = w_ref[2:3, :]
    b = b_ref[...]
    out = []
    for q in range(NP):
        prev = raw[q - 1] if q > 0 else before
        nxt = raw[q + 1] if q < NP - 1 else after
        out.append(prev * w0 + raw[q] * w1 + nxt * w2 + b)
    return out


def _longconv_kernel(x_ref, g_ref, wx_ref, bx_ref, wg_ref, bg_ref, bias_ref,
                     hr_ref, hi_ref, twc_ref, tws_ref, fcs_ref, ginv_ref, o_ref,
                     a_ref, b_ref, *, conv_x):
    rows = x_ref.shape[0] // NP
    kp = hr_ref.shape[1]
    kf = fcs_ref.shape[0] // 2
    xs = _phases(x_ref, rows)
    if conv_x:
        xs = _short_conv_phases(xs, wx_ref, bx_ref)
    gs = _short_conv_phases(_phases(g_ref, rows), wg_ref, bg_ref)

    xcat = jnp.concatenate([x.astype(BF16) for x in xs], axis=1)
    a_ref[...] = jnp.dot(fcs_ref[...], xcat, preferred_element_type=F32)

    def body(i):
        r0 = i * SUBLANES
        rs = pl.ds(r0, SUBLANES)
        zs = []
        for q in range(NP):
            ar = a_ref[rs, q * LANES:(q + 1) * LANES]
            ai = a_ref[pl.ds(kf + r0, SUBLANES), q * LANES:(q + 1) * LANES]
            zs.append((ar, ai) if q == 0 else
                      _cmul_tw(ar, ai, twc_ref[q, rs, :], tws_ref[q, rs, :], False))
        zs = _fft_list(zs, -1)
        ys = []
        for k2 in range(NP):
            zr, zi = zs[k2]
            hr = hr_ref[k2, rs, :]
            hi = hi_ref[k2, rs, :]
            ys.append((hr * zr - hi * zi, hr * zi + hi * zr))
        ys = _fft_list(ys, 1)
        for p in range(NP):
            br, bi = ys[p]
            if p > 0:
                br, bi = _cmul_tw(br, bi, twc_ref[p, rs, :], tws_ref[p, rs, :], True)
            b_ref[rs, p * LANES:(p + 1) * LANES] = br
            b_ref[pl.ds(kf + r0, SUBLANES), p * LANES:(p + 1) * LANES] = bi

    for i in range(kp // SUBLANES):
        body(i)
    if kf > kp:
        zeros = jnp.zeros((kf - kp, NP * LANES), F32)
        b_ref[kp:kf, :] = zeros
        b_ref[kf + kp:, :] = zeros

    y = jnp.dot(ginv_ref[...], b_ref[...].astype(BF16), preferred_element_type=F32)
    bias = bias_ref[...]
    for p in range(NP):
        yp = y[:, p * LANES:(p + 1) * LANES]
        o_ref[pl.ds(p, rows, stride=NP), :] = gs[p] * (yp + xs[p] * bias)


def _longconv(x3, x_blk0, hy3, g_blk0, sw, sb, wx_blk0, bias, h_blk0, hr, hi,
              twc, tws, fcs, ginv, conv_x):
    batch, seq, _ = hy3.shape
    nch = bias.shape[1] // LANES
    kp = hr.shape[1]
    kernel = functools.partial(_longconv_kernel, conv_x=conv_x)
    return pl.pallas_call(
        kernel,
        grid=(nch, batch),
        in_specs=[
            pl.BlockSpec((None, seq, LANES), lambda c, b: (b, 0, x_blk0 + c)),
            pl.BlockSpec((None, seq, LANES), lambda c, b: (b, 0, g_blk0 + c)),
            pl.BlockSpec((3, LANES), lambda c, b: (0, wx_blk0 + c)),
            pl.BlockSpec((1, LANES), lambda c, b: (0, wx_blk0 + c)),
            pl.BlockSpec((3, LANES), lambda c, b: (0, g_blk0 + c)),
            pl.BlockSpec((1, LANES), lambda c, b: (0, g_blk0 + c)),
            pl.BlockSpec((1, LANES), lambda c, b: (0, c)),
            pl.BlockSpec((NP, kp, LANES), lambda c, b: (0, 0, h_blk0 + c),
                         pipeline_mode=pl.Buffered(1)),
            pl.BlockSpec((NP, kp, LANES), lambda c, b: (0, 0, h_blk0 + c),
                         pipeline_mode=pl.Buffered(1)),
            _const_spec(twc.shape), _const_spec(tws.shape),
            _const_spec(fcs.shape), _const_spec(ginv.shape),
        ],
        out_specs=pl.BlockSpec((None, seq, LANES), lambda c, b: (b, 0, c)),
        out_shape=jax.ShapeDtypeStruct((batch, seq, nch * LANES), F32),
        scratch_shapes=[pltpu.VMEM((fcs.shape[0], NP * LANES), F32),
                        pltpu.VMEM((fcs.shape[0], NP * LANES), F32)],
        compiler_params=_cparams(("arbitrary", "arbitrary")),
        name="longconv_x" if conv_x else "longconv",
    )(x3, hy3, sw, sb, sw, sb, bias, hr, hi, twc, tws, fcs, ginv)


def _merge_kernel(x_ref, ya_ref, yb_ref, gpre_ref, gpost_ref, wgate_ref,
                  wa_ref, wb_ref, wo_ref, o_ref):
    d = x_ref.shape[-1]
    for rows in _row_halves(x_ref):
        x = x_ref[rows, :]
        u = _rms(x, gpre_ref[...]).astype(BF16)
        gates = jnp.dot(u, wgate_ref[...], preferred_element_type=F32)
        ma = jnp.dot(ya_ref[rows, :].astype(BF16), wa_ref[...], preferred_element_type=F32)
        mb = jnp.dot(yb_ref[rows, :], wb_ref[...], preferred_element_type=F32)
        m = _sigmoid(gates[:, :d]) * ma + _sigmoid(gates[:, d:]) * mb
        y = jnp.dot(m.astype(BF16), wo_ref[...], preferred_element_type=F32)
        o_ref[rows, :] = x + _rms(y, gpost_ref[...])


def _merge(x2d, ya, yb, gpre, gpost, wgate, wa, wb, wo, tm):
    n, d = x2d.shape
    row = lambda i: (i, 0)
    return pl.pallas_call(
        _merge_kernel,
        grid=(n // tm,),
        in_specs=[
            pl.BlockSpec((tm, d), row),
            pl.BlockSpec((tm, ya.shape[1]), row),
            pl.BlockSpec((tm, yb.shape[1]), row),
            _const_spec((1, d)), _const_spec((1, d)),
            _const_spec(wgate.shape), _const_spec(wa.shape),
            _const_spec(wb.shape), _const_spec(wo.shape),
        ],
        out_specs=pl.BlockSpec((tm, d), row),
        out_shape=jax.ShapeDtypeStruct((n, d), F32),
        compiler_params=_cparams(("parallel",)),
        name="merge",
    )(x2d, ya, yb, gpre, gpost, wgate, wa, wb, wo)


def _ffn_ple_kernel(x_ref, gpre_ref, gpost_ref, wg_ref, wu_ref, wd_ref,
                    p_ref, ppre_ref, ppost_ref, wpg_ref, wpp_ref, o_ref):
    x1 = _ffn_block(x_ref[...], gpre_ref, gpost_ref, wg_ref, wu_ref, wd_ref)
    u = _rms(x1, ppre_ref[...]).astype(BF16)
    g = _sigmoid(jnp.dot(u, wpg_ref[...], preferred_element_type=F32))
    e = jnp.dot(p_ref[...].astype(BF16), wpp_ref[...], preferred_element_type=F32)
    o_ref[...] = x1 + _rms(g * e, ppost_ref[...])


def _ffn_ple(x2d, gpre, gpost, wg, wu, wd, p2d, ppre, ppost, wpg, wpp, tm):
    n, d = x2d.shape
    row = lambda i: (i, 0)
    return pl.pallas_call(
        _ffn_ple_kernel,
        grid=(n // tm,),
        in_specs=[
            pl.BlockSpec((tm, d), row),
            _const_spec((1, d)), _const_spec((1, d)),
            _const_spec(wg.shape), _const_spec(wu.shape), _const_spec(wd.shape),
            pl.BlockSpec((tm, p2d.shape[1]), row),
            _const_spec((1, d)), _const_spec((1, d)),
            _const_spec(wpg.shape), _const_spec(wpp.shape),
        ],
        out_specs=pl.BlockSpec((tm, d), row),
        out_shape=jax.ShapeDtypeStruct((n, d), F32),
        compiler_params=_cparams(("parallel",)),
        name="ffn_ple",
    )(x2d, gpre, gpost, wg, wu, wd, p2d, ppre, ppost, wpg, wpp)


def _rope_tables(seq):
    rows = seq // GRID_W
    row = np.repeat(np.arange(rows, dtype=np.float32), GRID_W)
    col = np.tile(np.arange(GRID_W, dtype=np.float32), rows)
    inv = (ROPE_THETA ** (-np.arange(0, AXIS_DIM, 2, dtype=np.float32) / AXIS_DIM)
           ).astype(np.float32)
    ang = np.concatenate([row[:, None] * inv, col[:, None] * inv], axis=-1)
    cos = np.repeat(np.cos(ang), 2, axis=1)
    sin = np.repeat(np.sin(ang), 2, axis=1)
    sign = np.tile(np.array([-1.0, 1.0], np.float32), HEAD_DIM // 2)
    cos_t = np.tile(cos, (1, N_KV_HEADS)).astype(np.float32)
    sin_t = np.tile(sin * sign, (1, N_KV_HEADS)).astype(np.float32)
    return jnp.asarray(cos_t), jnp.asarray(sin_t)


def _block_diag_mean(width):
    idx = np.arange(width) // HEAD_DIM
    return jnp.asarray((idx[:, None] == idx[None, :]).astype(np.float32) / HEAD_DIM,
                       dtype=BF16)


def _filter_features(seq, pad_to):
    t = np.linspace(0.0, 1.0, seq, dtype=np.float32)[:, None]
    w = ((2.0 * math.pi / seq) * np.arange(seq, dtype=np.float32)).astype(np.float32)
    bands = np.linspace(1e-4, FILT_BANDS - 1, FILT_BANDS, dtype=np.float32)
    ang = w[:, None] * bands[None, :]
    z = np.concatenate([t, np.cos(ang), -np.sin(ang)], axis=-1).astype(np.float32)
    return jnp.asarray(np.pad(z, ((0, 0), (0, pad_to - z.shape[1]))))


def _dft_tables(seq):
    m1 = seq // NP
    nf = 2 * m1
    nk = nf // 2 + 1
    kp = -(-nk // SUBLANES) * SUBLANES
    half_tile = LANES // 2
    kf = -(-nk // half_tile) * half_tile
    k = np.arange(nk, dtype=np.float64)[:, None]
    m = np.arange(m1, dtype=np.float64)[None, :]
    th = 2.0 * np.pi * k * m / nf
    fcs = np.zeros((2 * kf, m1), np.float64)
    fcs[:nk] = np.cos(th)
    fcs[kf:kf + nk] = -np.sin(th)
    ck = np.full((nk,), 2.0 / nf)
    ck[0] = 1.0 / nf
    ck[-1] = 1.0 / nf
    ginv = np.zeros((m1, 2 * kf), np.float64)
    ginv[:, :nk] = (np.cos(th) * ck[:, None]).T
    ginv[:, kf:kf + nk] = (-np.sin(th) * ck[:, None]).T
    th1 = 2.0 * np.pi * np.arange(kp, dtype=np.float64) / nf
    th1[nk:] = 0.0
    cos_k = np.repeat(np.cos(th1)[:, None], LANES, axis=1)
    sin_k = np.repeat(np.sin(th1)[:, None], LANES, axis=1)
    thq = th1[None, :] * np.arange(NP, dtype=np.float64)[:, None] / NP
    twc = np.repeat(np.cos(thq)[:, :, None], LANES, axis=2)
    tws = np.repeat(np.sin(thq)[:, :, None], LANES, axis=2)
    return tuple(jnp.asarray(t, dtype=F32) for t in (fcs, ginv, cos_k, sin_k, twc, tws))


def _pick_tile(n, want):
    t = min(n, want)
    while n % t:
        t //= 2
    return t


def kernel(x, p, ffn1_norm_pre, ffn1_norm_post, ffn1_w_gate, ffn1_w_up, ffn1_w_down,
           mix_norm_pre, mix_norm_post, w_in, hy_short_w, hy_short_b,
           filt_w1, filt_b1, filt_freq1, filt_w2, filt_b2, filt_freq2, filt_w3,
           filt_deltas, hy_bias, q_norm, k_norm, w_hy_out, w_att_out, w_out,
           ffn2_norm_pre, ffn2_norm_post, ffn2_w_gate, ffn2_w_up, ffn2_w_down,
           ple_norm_pre, ple_norm_post, w_ple_gate, w_ple_proj):
    batch, seq, d = x.shape
    depth = p.shape[0]
    n = batch * seq
    nhy = 3 * HY_WIDTH
    nmix = nhy + ATT_Q + 2 * ATT_KV
    hy_blocks = HY_WIDTH // LANES

    tm = _pick_tile(seq, 512)
    tm_wide = _pick_tile(seq, 1024)
    tq = _pick_tile(seq, 512)
    cos_t, sin_t = _rope_tables(seq)
    bdq = _block_diag_mean(ATT_Q)
    bdk = _block_diag_mean(ATT_KV)
    ze = LANES
    zf = _filter_features(seq, ze)
    row = lambda a: a.reshape(1, -1).astype(F32)
    bf = lambda a: a.astype(BF16)
    fcs, ginv, cos_k, sin_k, twc, tws = _dft_tables(seq)
    fcs, ginv = bf(fcs), bf(ginv)

    x2d = x.reshape(n, d)
    for i in range(depth):
        x2d = _ffn(x2d, row(ffn1_norm_pre[i]), row(ffn1_norm_post[i]),
                   bf(ffn1_w_gate[i]), bf(ffn1_w_up[i]), bf(ffn1_w_down[i]), tm)

        hy, q, k, v = _inproj(
            x2d, row(mix_norm_pre[i]), bf(w_in[i][:, :nmix]),
            row(jnp.tile(q_norm[i], N_Q_HEADS)), row(jnp.tile(k_norm[i], N_KV_HEADS)),
            bdq, bdk, cos_t, sin_t, seq, tm_wide)

        w3 = filt_w3[i].reshape(FILT_HID, HY_ORDER, 2, HY_WIDTH)
        dl = filt_deltas[i]
        w1p = jnp.pad(filt_w1[i], ((0, ze - FILT_EMB), (0, 0)))
        hf, hb = _filt_mlp(
            zf, w1p, row(filt_b1[i]), row(filt_freq1[i]), filt_w2[i],
            row(filt_b2[i]), row(filt_freq2[i]),
            w3[:, :, 0].reshape(FILT_HID, -1), w3[:, :, 1].reshape(FILT_HID, -1),
            row(dl[:, 0]), row(dl[:, 1]), LANES)
        hr, hi = _filt_dft(hf, hb, fcs, cos_k, sin_k, twc, tws)

        hy3 = hy.reshape(batch, seq, nhy)
        sw = hy_short_w[i]
        sb = row(hy_short_b[i])
        z1 = _longconv(hy3, 0, hy3, hy_blocks, sw, sb, 0, row(hy_bias[i][0]),
                       0, hr, hi, twc, tws, fcs, ginv, True)
        ya = _longconv(z1, 0, hy3, 2 * hy_blocks, sw, sb, 0, row(hy_bias[i][1]),
                       hy_blocks, hr, hi, twc, tws, fcs, ginv, False)

        yb = _attention(q, k, v, q_norm[i], k_norm[i], batch, seq, tq)

        x2d = _merge(x2d, ya.reshape(n, HY_WIDTH), yb, row(mix_norm_pre[i]),
                     row(mix_norm_post[i]), bf(w_in[i][:, nmix:]), bf(w_hy_out[i]),
                     bf(w_att_out[i]), bf(w_out[i]), tm_wide)

        x2d = _ffn_ple(x2d, row(ffn2_norm_pre[i]), row(ffn2_norm_post[i]),
                       bf(ffn2_w_gate[i]), bf(ffn2_w_up[i]), bf(ffn2_w_down[i]),
                       p[i].reshape(n, -1), row(ple_norm_pre[i]), row(ple_norm_post[i]),
                       bf(w_ple_gate[i]), bf(w_ple_proj[i]), tm)
    return x2d.reshape(batch, seq, d)
```
